```python
import math
import jax, jax.numpy as jnp
from jax import lax
import numpy as np

D_MODEL = 1024
BATCH = 2
SEQ = 16384
DEPTH = 2
DEC_BATCH = 4
DEC_SEQ = 4096
PAST_LEN = 128

GDN_HEADS = D_MODEL // 256
GDN_DK = 128
GDN_DV = 128
GDN_WIDTH = GDN_HEADS * GDN_DV
CONV_W = 5
CHUNK = 64
DIFF_HEADS = D_MODEL // 256
DIFF_DH = 64
DIFF_DV = 2 * DIFF_DH
DIFF_WIDTH = DIFF_HEADS * DIFF_DV
MIX_WIDTH = GDN_WIDTH + DIFF_WIDTH
D_FF = 4 * D_MODEL
Q_BLOCK = 128
EPS = 1e-6
SPLITS = [3 * GDN_WIDTH, GDN_WIDTH, 2 * GDN_HEADS, 2 * GDN_HEADS,
          2 * DIFF_HEADS * DIFF_DH, 2 * DIFF_HEADS * DIFF_DH, DIFF_WIDTH]
SPLIT_IDX = [int(v) for v in np.cumsum(SPLITS)[:-1]]
D_IN = int(sum(SPLITS))

kernel_name = "hymba_gdn_diffattn_encoder"


def rmsnorm(x, w):
    xf = x.astype(jnp.float32)
    y = xf * lax.rsqrt(jnp.mean(xf * xf, axis=-1, keepdims=True) + EPS)
    return (y * w.astype(jnp.float32)).astype(x.dtype)


def l2norm(x):
    return x * lax.rsqrt(jnp.sum(x * x, axis=-1, keepdims=True) + EPS)


def centred_conv(x, w):
    C = x.shape[-1]
    pad = CONV_W // 2
    return lax.conv_general_dilated(x, w[:, None, :].astype(x.dtype), window_strides=(1,),
                                    padding=[(pad, pad)], dimension_numbers=('NWC', 'WIO', 'NWC'),
                                    feature_group_count=C)


def gated_delta_chunked(q, k, v, g, beta):
    B, S, H, DK = q.shape
    DV = v.shape[-1]
    N = S // CHUNK

    def chunks(t):
        t = t.reshape((B, N, CHUNK, H) + t.shape[3:])
        return jnp.moveaxis(t, 3, 1)

    q, k, v, g, beta = chunks(q), chunks(k), chunks(v), chunks(g), chunks(beta)
    gc = jnp.cumsum(g, axis=-1)
    tri = jnp.tril(jnp.ones((CHUNK, CHUNK), dtype=bool))
    strict = jnp.tril(jnp.ones((CHUNK, CHUNK), dtype=bool), -1)
    decay = jnp.exp(jnp.where(tri, gc[..., :, None] - gc[..., None, :], -jnp.inf))
    kb = k * beta[..., None]
    a = jnp.einsum('bhncd,bhned->bhnce', kb, k) * decay
    a = jnp.where(strict, a, 0.0) + jnp.eye(CHUNK, dtype=a.dtype)
    rhs = jnp.concatenate([v * beta[..., None], kb * jnp.exp(gc)[..., None]], axis=-1)
    sol = lax.linalg.triangular_solve(a, rhs, left_side=True, lower=True, unit_diagonal=True)
    u, w = sol[..., :DV], sol[..., DV:]
    attn_qk = jnp.einsum('bhncd,bhned->bhnce', q, k) * decay
    q_dec = q * jnp.exp(gc)[..., None]
    k_dec = k * jnp.exp(gc[..., -1:] - gc)[..., None]
    g_last = jnp.exp(gc[..., -1])

    def step(state, xs):
        u_n, w_n, qd_n, kd_n, a_n, gl_n = xs
        v_new = u_n - jnp.einsum('bhcd,bhde->bhce', w_n, state)
        o = jnp.einsum('bhcd,bhde->bhce', qd_n, state) + jnp.einsum('bhce,bhef->bhcf', a_n, v_new)
        state = state * gl_n[..., None, None] + jnp.einsum('bhcd,bhce->bhde', kd_n, v_new)
        return state, o

    xs = tuple(jnp.moveaxis(t, 2, 0) for t in (u, w, q_dec, k_dec, attn_qk, g_last))
    s0 = jnp.zeros((B, H, DK, DV), jnp.float32)
    _, o = lax.scan(step, s0, xs)
    o = jnp.moveaxis(jnp.moveaxis(o, 0, 2), 1, 3)
    return o.reshape(B, S, H, DV)


def gdn_mixer(qkv, gate, a, b, conv_w, a_log, dt_bias, gdn_norm):
    B, S, _ = qkv.shape
    dt = qkv.dtype
    qkv = jax.nn.silu(centred_conv(qkv, conv_w)).astype(jnp.float32)
    q, k, v = jnp.split(qkv, 3, axis=-1)
    q = l2norm(q.reshape(B, S, GDN_HEADS, GDN_DK)) * (GDN_DK ** -0.5)
    k = l2norm(k.reshape(B, S, GDN_HEADS, GDN_DK))
    v = v.reshape(B, S, GDN_HEADS, GDN_DV)
    a = a.astype(jnp.float32).reshape(B, S, 2, GDN_HEADS)
    beta = jax.nn.sigmoid(b.astype(jnp.float32).reshape(B, S, 2, GDN_HEADS))
    g = -jnp.exp(a_log.astype(jnp.float32)) * jax.nn.softplus(a + dt_bias.astype(jnp.float32))
    o_fwd = gated_delta_chunked(q, k, v, g[:, :, 0], beta[:, :, 0])
    flip = lambda t: jnp.flip(t, axis=1)
    o_bwd = flip(gated_delta_chunked(flip(q), flip(k), flip(v), flip(g[:, :, 1]), flip(beta[:, :, 1])))
    o = rmsnorm(o_fwd + o_bwd, gdn_norm)
    o = o * jax.nn.silu(gate.astype(jnp.float32).reshape(B, S, GDN_HEADS, GDN_DV))
    return o.reshape(B, S, GDN_WIDTH).astype(dt)


def alibi_slopes(n_heads):
    return jnp.array([2.0 ** (-8.0 * (h + 1) / n_heads) for h in range(n_heads)], jnp.float32)


def diff_attention(q, k, v, lam, lambda_init, diff_norm):
    B, S, _ = q.shape
    q = q.reshape(B, S, DIFF_HEADS, 2, DIFF_DH)
    k = k.reshape(B, S, DIFF_HEADS, 2, DIFF_DH)
    v = v.reshape(B, S, DIFF_HEADS, DIFF_DV)
    scale = DIFF_DH ** -0.5
    slopes = alibi_slopes(DIFF_HEADS)
    pos = jnp.arange(S, dtype=jnp.float32)
    nb = S // Q_BLOCK
    qb = jnp.moveaxis(q.reshape(B, nb, Q_BLOCK, DIFF_HEADS, 2, DIFF_DH), 1, 0)
    pb = pos.reshape(nb, Q_BLOCK)

    def block(args):
        q_blk, p_blk = args
        s = jnp.einsum('bqhmd,bkhmd->bhmqk', q_blk, k).astype(jnp.float32) * scale
        bias = -slopes[:, None, None] * jnp.abs(p_blk[:, None] - pos[None, :])
        p = jax.nn.softmax(s + bias[:, None], axis=-1)
        wts = p[:, :, 0] - lam * p[:, :, 1]
        return jnp.einsum('bhqk,bkhe->bqhe', wts.astype(v.dtype), v)

    o = lax.map(block, (qb, pb))
    o = jnp.moveaxis(o, 0, 1).reshape(B, S, DIFF_HEADS, DIFF_DV)
    o = rmsnorm(o, diff_norm) * (1.0 - lambda_init)
    return o.reshape(B, S, DIFF_WIDTH)


def trunk(x, norm_mix, w_in, conv_w, a_log, dt_bias, gdn_norm, lambda_q1, lambda_k1,
          lambda_q2, lambda_k2, diff_norm, w_out, norm_mlp, w_up, w_down, norm_final):
    for l in range(DEPTH):
        lambda_init = 0.8 - 0.6 * math.exp(-0.3 * l)
        h = rmsnorm(x, norm_mix[l])
        z = h @ w_in[l]
        qkv, gate, a, b, dq, dk, dv = jnp.split(z, SPLIT_IDX, axis=-1)
        o_gdn = gdn_mixer(qkv, gate, a, b, conv_w[l], a_log[l], dt_bias[l], gdn_norm[l])
        lam = (jnp.exp(jnp.sum(lambda_q1[l].astype(jnp.float32) * lambda_k1[l].astype(jnp.float32)))
               - jnp.exp(jnp.sum(lambda_q2[l].astype(jnp.float32) * lambda_k2[l].astype(jnp.float32)))
               + lambda_init)
        o_diff = diff_attention(dq, dk, dv, lam, lambda_init, diff_norm[l])
        x = x + jnp.concatenate([o_gdn, o_diff.astype(o_gdn.dtype)], axis=-1) @ w_out[l]
        h = rmsnorm(x, norm_mlp[l])
        x = x + jnp.square(jax.nn.relu(h @ w_up[l])) @ w_down[l]
    return rmsnorm(x, norm_final)


def setup_inputs(seed: int = 0) -> dict:
    key = jax.random.key(seed)
    ks = jax.random.split(key, 20)
    nrm = lambda k, shape, s: jax.random.normal(k, shape, jnp.float32) * s
    gain = lambda k, shape: 1.0 + 0.02 * jax.random.normal(k, shape, jnp.float32)
    dt0 = jnp.exp(jax.random.uniform(ks[5], (DEPTH, 2, GDN_HEADS), jnp.float32,
                                     math.log(1e-3), math.log(1e-1)))
    return {
        "x_prompt": jax.random.normal(ks[0], (BATCH, SEQ, D_MODEL), jnp.float32),
        "x_sample": jax.random.normal(ks[1], (DEC_BATCH, DEC_SEQ, D_MODEL), jnp.float32),
        "norm_mix": gain(ks[2], (DEPTH, D_MODEL)),
        "w_in": nrm(ks[3], (DEPTH, D_MODEL, D_IN), D_MODEL ** -0.5),
        "conv_w": nrm(ks[4], (DEPTH, CONV_W, 3 * GDN_WIDTH), CONV_W ** -0.5),
        "a_log": jnp.log(jax.random.uniform(ks[6], (DEPTH, 2, GDN_HEADS), jnp.float32, 1.0, 16.0)),
        "dt_bias": dt0 + jnp.log(-jnp.expm1(-dt0)),
        "gdn_norm": gain(ks[7], (DEPTH, GDN_DV)),
        "lambda_q1": nrm(ks[8], (DEPTH, DIFF_DH), 0.1),
        "lambda_k1": nrm(ks[9], (DEPTH, DIFF_DH), 0.1),
        "lambda_q2": nrm(ks[10], (DEPTH, DIFF_DH), 0.1),
        "lambda_k2": nrm(ks[11], (DEPTH, DIFF_DH), 0.1),
        "diff_norm": gain(ks[12], (DEPTH, DIFF_DV)),
        "w_out": nrm(ks[13], (DEPTH, MIX_WIDTH, D_MODEL), MIX_WIDTH ** -0.5),
        "norm_mlp": gain(ks[14], (DEPTH, D_MODEL)),
        "w_up": nrm(ks[15], (DEPTH, D_MODEL, D_FF), D_MODEL ** -0.5),
        "w_down": nrm(ks[16], (DEPTH, D_FF, D_MODEL), 0.5 * D_FF ** -0.5),
        "norm_final": gain(ks[17], (D_MODEL,)),
    }


def reference(x_prompt, x_sample, norm_mix, w_in, conv_w, a_log, dt_bias, gdn_norm, lambda_q1,
              lambda_k1, lambda_q2, lambda_k2, diff_norm, w_out, norm_mlp, w_up, w_down, norm_final):
    y_prompt = trunk(x_prompt, norm_mix, w_in, conv_w, a_log, dt_bias, gdn_norm, lambda_q1, lambda_k1,
                     lambda_q2, lambda_k2, diff_norm, w_out, norm_mlp, w_up, w_down, norm_final)
    y_sample = trunk(x_sample, norm_mix, w_in, conv_w, a_log, dt_bias, gdn_norm, lambda_q1, lambda_k1,
                     lambda_q2, lambda_k2, diff_norm, w_out, norm_mlp, w_up, w_down, norm_final)
    return (y_prompt, y_sample)
```

```python
import functools
import math

import jax
import jax.numpy as jnp
from jax import lax
from jax.experimental import pallas as pl
from jax.experimental.pallas import tpu as pltpu

D_MODEL = 1024
DEPTH = 2
GDN_HEADS = 4
GDN_DK = 128
GDN_DV = 128
GDN_WIDTH = GDN_HEADS * GDN_DV
CONV_W = 5
CONV_PAD = CONV_W // 2
CHUNK = 64
DIFF_HEADS = 4
DIFF_DH = 64
DIFF_DV = 2 * DIFF_DH
DIFF_WIDTH = DIFF_HEADS * DIFF_DV
D_FF = 4 * D_MODEL
EPS = 1e-6
QKV_W = 3 * GDN_WIDTH
AB_LO = QKV_W + GDN_WIDTH
AB_HI = AB_LO + 4 * GDN_HEADS
LANES = 128
SUBLANES = 8
VMEM_LIMIT = 56 * 1024 * 1024
NEG_BIG = -1e30

BF16 = jnp.bfloat16
F32 = jnp.float32


def _dot(a, b):
    return jnp.dot(a, b, preferred_element_type=F32)


def _dot_nt(a, b):
    return lax.dot_general(a, b, (((1,), (1,)), ((), ())), preferred_element_type=F32)


def _dot_tn(a, b):
    return lax.dot_general(a, b, (((0,), (0,)), ((), ())), preferred_element_type=F32)


def _sigmoid(x):
    return 1.0 / (1.0 + jnp.exp(-x))


def _params(sem):
    return pltpu.CompilerParams(dimension_semantics=sem, vmem_limit_bytes=VMEM_LIMIT)


def _inproj_kernel(x_ref, nw_ref, wm_ref, wab_ref,
                   qkv_ref, gate_ref, ab_ref, dq_ref, dk_ref, dv_ref):
    x = x_ref[...]
    ms = jnp.mean(x * x, axis=-1, keepdims=True)
    h = (x * lax.rsqrt(ms + EPS) * nw_ref[...]).astype(BF16)
    qkv_ref[...] = _dot(h, wm_ref[:, 0:QKV_W])
    gate_ref[...] = _dot(h, wm_ref[:, QKV_W:AB_LO])
    o = AB_LO
    dq_ref[...] = (_dot(h, wm_ref[:, o:o + DIFF_WIDTH]) * (DIFF_DH ** -0.5)).astype(BF16)
    dk_ref[...] = _dot(h, wm_ref[:, o + DIFF_WIDTH:o + 2 * DIFF_WIDTH]).astype(BF16)
    dv_ref[...] = _dot(h, wm_ref[:, o + 2 * DIFF_WIDTH:o + 3 * DIFF_WIDTH]).astype(BF16)
    ab_ref[...] = _dot(h, wab_ref[...])


def _inproj(x2d, nw, wm, wab):
    T = x2d.shape[0]
    tm = min(512, T)
    row = lambda i: (i, 0)
    fixed = lambda i: (0, 0)
    nmain = wm.shape[1]
    return pl.pallas_call(
        _inproj_kernel,
        grid=(T // tm,),
        in_specs=[pl.BlockSpec((tm, D_MODEL), row),
                  pl.BlockSpec((1, D_MODEL), fixed),
                  pl.BlockSpec((D_MODEL, nmain), fixed),
                  pl.BlockSpec((D_MODEL, LANES), fixed)],
        out_specs=[pl.BlockSpec((tm, QKV_W), row),
                   pl.BlockSpec((tm, GDN_WIDTH), row),
                   pl.BlockSpec((tm, LANES), row),
                   pl.BlockSpec((tm, DIFF_WIDTH), row),
                   pl.BlockSpec((tm, DIFF_WIDTH), row),
                   pl.BlockSpec((tm, DIFF_WIDTH), row)],
        out_shape=[jax.ShapeDtypeStruct((T, QKV_W), F32),
                   jax.ShapeDtypeStruct((T, GDN_WIDTH), F32),
                   jax.ShapeDtypeStruct((T, LANES), F32),
                   jax.ShapeDtypeStruct((T, DIFF_WIDTH), BF16),
                   jax.ShapeDtypeStruct((T, DIFF_WIDTH), BF16),
                   jax.ShapeDtypeStruct((T, DIFF_WIDTH), BF16)],
        compiler_params=_params(("parallel",)),
        name="inproj",
    )(x2d, nw, wm, wab)


def _gdn_prep_kernel(cur_ref, prev_ref, next_ref, ab_ref, cw_ref, gp_ref,
                     q_ref, k_ref, v_ref, g_ref, xpad):
    i = pl.program_id(1)
    n = pl.num_programs(1)
    tm = cur_ref.shape[1]
    halo = SUBLANES
    xpad[halo:halo + tm, :] = cur_ref[0]
    xpad[0:halo, :] = jnp.where(i > 0, prev_ref[0], 0.0)
    xpad[halo + tm:2 * halo + tm, :] = jnp.where(i < n - 1, next_ref[0], 0.0)

    outs = (q_ref, k_ref, v_ref)
    for grp in range(3):
        c0 = grp * GDN_WIDTH
        acc = None
        for kk in range(CONV_W):
            r0 = halo - CONV_PAD + kk
            term = xpad[r0:r0 + tm, c0:c0 + GDN_WIDTH] * cw_ref[kk:kk + 1, c0:c0 + GDN_WIDTH]
            acc = term if acc is None else acc + term
        y = acc * _sigmoid(acc)
        if grp < 2:
            parts = []
            for h in range(GDN_HEADS):
                yh = y[:, h * GDN_DK:(h + 1) * GDN_DK]
                yh = yh * lax.rsqrt(jnp.sum(yh * yh, axis=-1, keepdims=True) + EPS)
                if grp == 0:
                    yh = yh * (GDN_DK ** -0.5)
                parts.append(yh)
            y = jnp.concatenate(parts, axis=1)
        outs[grp][0] = y

    ab = ab_ref[0]
    a_coef = -jnp.exp(gp_ref[0:1, :])
    z = ab + gp_ref[1:2, :]
    softplus = jnp.maximum(z, 0.0) + jnp.log1p(jnp.exp(-jnp.abs(z)))
    lane = lax.broadcasted_iota(jnp.int32, ab.shape, 1)
    ng = 2 * GDN_HEADS
    g_ref[0] = jnp.where(lane < ng, a_coef * softplus,
                         jnp.where(lane < 2 * ng, _sigmoid(ab), 0.0))


def _gdn_prep(qkv, ab, cw, gp):
    B, S, _ = qkv.shape
    tm = min(256, S)
    nb = tm // SUBLANES
    last = S // SUBLANES - 1
    tok = lambda b, i: (b, i, 0)
    fixed = lambda b, i: (0, 0)
    return pl.pallas_call(
        _gdn_prep_kernel,
        grid=(B, S // tm),
        in_specs=[pl.BlockSpec((1, tm, QKV_W), tok),
                  pl.BlockSpec((1, SUBLANES, QKV_W), lambda b, i: (b, jnp.maximum(i * nb - 1, 0), 0)),
                  pl.BlockSpec((1, SUBLANES, QKV_W), lambda b, i: (b, jnp.minimum((i + 1) * nb, last), 0)),
                  pl.BlockSpec((1, tm, LANES), tok),
                  pl.BlockSpec((CONV_W, QKV_W), fixed),
                  pl.BlockSpec((2, LANES), fixed)],
        out_specs=[pl.BlockSpec((1, tm, GDN_WIDTH), tok)] * 3 + [pl.BlockSpec((1, tm, LANES), tok)],
        out_shape=[jax.ShapeDtypeStruct((B, S, GDN_WIDTH), F32)] * 3
                  + [jax.ShapeDtypeStruct((B, S, LANES), F32)],
        scratch_shapes=[pltpu.VMEM((tm + 2 * SUBLANES, QKV_W), F32)],
        compiler_params=_params(("parallel", "parallel")),
        name="gdn_prep",
    )(qkv, qkv, qkv, ab, cw, gp)


def _gdn_intra_kernel(q_ref, k_ref, v_ref, g_ref,
                      u_ref, w_ref, qd_ref, kd_ref, at_ref, gl_ref):
    tc = q_ref.shape[1]
    C = CHUNK
    H = GDN_HEADS
    G = g_ref[0]
    pos = lax.broadcasted_iota(jnp.int32, G.shape, 0) & (C - 1)
    gcf = G
    gcb = G
    s = 1
    while s < C:
        gcf = gcf + jnp.where(pos >= s, pltpu.roll(gcf, s, axis=0), 0.0)
        gcb = gcb + jnp.where(pos < C - s, pltpu.roll(gcb, tc - s, axis=0), 0.0)
        s *= 2

    ri = lax.broadcasted_iota(jnp.int32, (C, C), 0)
    ci = lax.broadcasted_iota(jnp.int32, (C, C), 1)
    eye = (ri == ci).astype(F32)

    for c in range(tc // C):
        r0 = c * C
        for h in range(H):
            qh = q_ref[0, r0:r0 + C, h * GDN_DK:(h + 1) * GDN_DK]
            kh = k_ref[0, r0:r0 + C, h * GDN_DK:(h + 1) * GDN_DK]
            vh = v_ref[0, r0:r0 + C, h * GDN_DV:(h + 1) * GDN_DV]
            khb = kh.astype(BF16)
            for d in range(2):
                ln = d * H + h
                gc_all = gcf if d == 0 else gcb
                g_col = G[r0:r0 + C, ln:ln + 1]
                gc_col = gc_all[r0:r0 + C, ln:ln + 1]
                beta_col = G[r0:r0 + C, 2 * H + ln:2 * H + ln + 1]
                last = C - 1 if d == 0 else 0
                gc_last = gc_all[r0 + last:r0 + last + 1, ln:ln + 1]
                if d == 0:
                    incl = ci <= ri
                    strict = ci < ri
                    colsum_mask = ri <= ci
                else:
                    incl = ci >= ri
                    strict = ci > ri
                    colsum_mask = ri >= ci
                gc_row = jnp.sum(jnp.where(colsum_mask, g_col, 0.0), axis=0, keepdims=True)
                decay = jnp.exp(jnp.where(incl, gc_col - gc_row, NEG_BIG))
                e_col = jnp.exp(gc_col)
                kb = kh * beta_col
                m1 = _dot_nt(jnp.concatenate([kb, qh], axis=0).astype(BF16), khb)
                a = jnp.where(strict, m1[:C] * decay, 0.0)
                attn = m1[C:] * decay
                p = eye - a
                apow = a.astype(BF16)
                n = 2
                while n < C:
                    apow_f = _dot(apow, apow)
                    apow = apow_f.astype(BF16)
                    p = p + _dot(p.astype(BF16), apow)
                    n *= 2
                rhs = jnp.concatenate([vh * beta_col, kb * e_col], axis=1).astype(BF16)
                sol = _dot(p.astype(BF16), rhs)
                lo = ln * GDN_DK
                u_ref[0, r0:r0 + C, lo:lo + GDN_DV] = sol[:, :GDN_DV]
                w_ref[0, r0:r0 + C, lo:lo + GDN_DK] = sol[:, GDN_DV:].astype(BF16)
                qd_ref[0, r0:r0 + C, lo:lo + GDN_DK] = (qh * e_col).astype(BF16)
                kd_ref[0, r0:r0 + C, lo:lo + GDN_DK] = (kh * jnp.exp(gc_last - gc_col)).astype(BF16)
                at_ref[0, r0:r0 + C, ln * C:(ln + 1) * C] = attn.astype(BF16)
                gl_ref[0, c, ln:ln + 1, :] = jnp.broadcast_to(jnp.exp(gc_last), (1, LANES))


def _gdn_intra(q, k, v, g):
    B, S, _ = q.shape
    tc = min(128, S)
    tok = lambda b, i: (b, i, 0)
    wide = 2 * GDN_WIDTH
    return pl.pallas_call(
        _gdn_intra_kernel,
        grid=(B, S // tc),
        in_specs=[pl.BlockSpec((1, tc, GDN_WIDTH), tok)] * 3 + [pl.BlockSpec((1, tc, LANES), tok)],
        out_specs=[pl.BlockSpec((1, tc, wide), tok)] * 4
                  + [pl.BlockSpec((1, tc, 2 * GDN_HEADS * CHUNK), tok),
                     pl.BlockSpec((1, tc // CHUNK, 2 * GDN_HEADS, LANES), lambda b, i: (b, i, 0, 0))],
        out_shape=[jax.ShapeDtypeStruct((B, S, wide), F32)]
                  + [jax.ShapeDtypeStruct((B, S, wide), BF16)] * 3
                  + [jax.ShapeDtypeStruct((B, S, 2 * GDN_HEADS * CHUNK), BF16),
                     jax.ShapeDtypeStruct((B, S // CHUNK, 2 * GDN_HEADS, LANES), F32)],
        compiler_params=_params(("parallel", "parallel")),
        name="gdn_intra",
    )(q, k, v, g)


def _gdn_scan_kernel(uf_ref, ub_ref, wf_ref, wb_ref, qf_ref, qb_ref, kf_ref, kb_ref,
                     af_ref, ab_ref, gf_ref, gb_ref, of_ref, ob_ref, st_ref):
    n = pl.program_id(1)
    tb = uf_ref.shape[1]
    C = CHUNK
    H = GDN_HEADS
    nc = tb // C

    @pl.when(n == 0)
    def _():
        st_ref[...] = jnp.zeros_like(st_ref)

    refs = ((uf_ref, wf_ref, qf_ref, kf_ref, af_ref, gf_ref, of_ref),
            (ub_ref, wb_ref, qb_ref, kb_ref, ab_ref, gb_ref, ob_ref))
    for cc in range(nc):
        for d in range(2):
            u_ref, w_ref, qd_ref, kd_ref, at_ref, gl_ref, o_ref = refs[d]
            c = cc if d == 0 else nc - 1 - cc
            r0 = c * C
            for h in range(H):
                ln = d * H + h
                cols = slice(h * GDN_DK, (h + 1) * GDN_DK)
                state = st_ref[ln]
                wq = jnp.concatenate([w_ref[0, r0:r0 + C, cols], qd_ref[0, r0:r0 + C, cols]], axis=0)
                m = _dot(wq, state.astype(BF16))
                v_new = u_ref[0, r0:r0 + C, cols] - m[:C]
                vb = v_new.astype(BF16)
                o = m[C:] + _dot(at_ref[0, r0:r0 + C, h * C:(h + 1) * C], vb)
                o_ref[0, r0:r0 + C, cols] = o
                gl = gl_ref[0, c, ln:ln + 1, :]
                st_ref[ln] = state * gl + _dot_tn(kd_ref[0, r0:r0 + C, cols], vb)


def _gdn_scan(u, w, qd, kd, at, gl):
    B, S, _ = u.shape
    tb = min(256, S)
    nb = S // tb
    nc = tb // CHUNK
    fwd = lambda b, n: (b, n, 0)
    bwd = lambda b, n: (b, nb - 1 - n, 1)
    fwd4 = lambda b, n: (b, n, 0, 0)
    bwd4 = lambda b, n: (b, nb - 1 - n, 0, 0)
    wide = GDN_WIDTH
    aw = GDN_HEADS * CHUNK
    pair = lambda shape: [pl.BlockSpec(shape, fwd), pl.BlockSpec(shape, bwd)]
    glshape = (1, nc, 2 * GDN_HEADS, LANES)
    return pl.pallas_call(
        _gdn_scan_kernel,
        grid=(B, nb),
        in_specs=pair((1, tb, wide)) * 4 + pair((1, tb, aw))
                 + [pl.BlockSpec(glshape, fwd4), pl.BlockSpec(glshape, bwd4)],
        out_specs=[pl.BlockSpec((1, tb, wide), fwd),
                   pl.BlockSpec((1, tb, wide), lambda b, n: (b, nb - 1 - n, 0))],
        out_shape=[jax.ShapeDtypeStruct((B, S, wide), F32)] * 2,
        scratch_shapes=[pltpu.VMEM((2 * GDN_HEADS, GDN_DK, GDN_DV), F32)],
        compiler_params=_params(("parallel", "arbitrary")),
        name="gdn_scan",
    )(u, u, w, w, qd, qd, kd, kd, at, at, gl, gl)


def _attn_kernel(lambda_init, tk, q_ref, k_ref, v_ref, ns_ref, lp_ref, dn_ref, o_ref,
                 q2_scr, rel_scr, m_scr, l_scr, acc_scr):
    i = pl.program_id(2)
    tq = q_ref.shape[1]
    S = k_ref.shape[1]
    q = q_ref[0]
    lane = lax.broadcasted_iota(jnp.int32, q.shape, 1)
    zero = jnp.zeros_like(q)
    q2_scr[0:tq, :] = jnp.where(lane < DIFF_DH, q, zero)
    q2_scr[tq:2 * tq, :] = jnp.where(lane >= DIFF_DH, q, zero)
    rel_scr[...] = (lax.broadcasted_iota(jnp.int32, (tq, tk), 0)
                    - lax.broadcasted_iota(jnp.int32, (tq, tk), 1)).astype(F32)
    m_scr[...] = jnp.full_like(m_scr, NEG_BIG)
    l_scr[...] = jnp.zeros_like(l_scr)
    acc_scr[...] = jnp.zeros_like(acc_scr)
    nslope = ns_ref[0, 0:1, 0:1]

    def body(j, carry):
        j0 = pl.multiple_of(j * tk, tk)
        kt = k_ref[0, pl.ds(j0, tk), :]
        vt = v_ref[0, pl.ds(j0, tk), :]
        s = _dot_nt(q2_scr[...], kt)
        d0 = (i * tq - j0).astype(F32)
        bias = nslope * jnp.abs(rel_scr[...] + d0)
        s = s + jnp.concatenate([bias, bias], axis=0)
        m_prev = m_scr[...]
        m_new = jnp.maximum(m_prev, jnp.max(s, axis=1, keepdims=True))
        alpha = jnp.exp(m_prev - m_new)
        p = jnp.exp(s - jnp.tile(m_new, (1, tk // LANES)))
        l_scr[...] = alpha * l_scr[...] + jnp.sum(p, axis=1, keepdims=True)
        acc_scr[...] = alpha * acc_scr[...] + _dot(p.astype(BF16), vt)
        m_scr[...] = m_new
        return carry

    lax.fori_loop(0, S // tk, body, 0)

    lp = lp_ref[...]
    lam = (jnp.exp(jnp.sum(lp[0:1] * lp[1:2], axis=1, keepdims=True))
           - jnp.exp(jnp.sum(lp[2:3] * lp[3:4], axis=1, keepdims=True)) + lambda_init)
    o_all = acc_scr[...] / l_scr[...]
    o = o_all[:tq] - lam * o_all[tq:]
    o = o * lax.rsqrt(jnp.mean(o * o, axis=-1, keepdims=True) + EPS) * dn_ref[...]
    o_ref[0] = (o * (1.0 - lambda_init)).astype(o_ref.dtype)


def _attention(dq, dk, dv, nslopes, lp, dn, lambda_init):
    B, S, _ = dq.shape
    tq = min(256, S)
    tk = min(512, S)
    qmap = lambda b, h, i: (b, i, h)
    kmap = lambda b, h, i: (b, 0, h)
    return pl.pallas_call(
        functools.partial(_attn_kernel, lambda_init, tk),
        grid=(B, DIFF_HEADS, S // tq),
        in_specs=[pl.BlockSpec((1, tq, DIFF_DV), qmap),
                  pl.BlockSpec((1, S, DIFF_DV), kmap),
                  pl.BlockSpec((1, S, DIFF_DV), kmap),
                  pl.BlockSpec((1, SUBLANES, LANES), lambda b, h, i: (h, 0, 0)),
                  pl.BlockSpec((4, DIFF_DH), lambda b, h, i: (0, 0)),
                  pl.BlockSpec((1, DIFF_DV), lambda b, h, i: (0, 0))],
        out_specs=pl.BlockSpec((1, tq, DIFF_DV), qmap),
        out_shape=jax.ShapeDtypeStruct((B, S, DIFF_WIDTH), BF16),
        scratch_shapes=[pltpu.VMEM((2 * tq, DIFF_DV), BF16),
                        pltpu.VMEM((tq, tk), F32),
                        pltpu.VMEM((2 * tq, LANES), F32),
                        pltpu.VMEM((2 * tq, LANES), F32),
                        pltpu.VMEM((2 * tq, DIFF_DV), F32)],
        compiler_params=_params(("parallel", "parallel", "parallel")),
        name="diff_attn",
    )(dq, dk, dv, nslopes, lp, dn)


def _mix_kernel(x_ref, of_ref, ob_ref, gate_ref, oa_ref, gn_ref, wo_ref, y_ref):
    o = of_ref[...] + ob_ref[...]
    gate = gate_ref[...]
    parts = []
    for h in range(GDN_HEADS):
        cols = slice(h * GDN_DV, (h + 1) * GDN_DV)
        oh = o[:, cols]
        oh = oh * lax.rsqrt(jnp.mean(oh * oh, axis=-1, keepdims=True) + EPS) * gn_ref[...]
        gh = gate[:, cols]
        parts.append(oh * (gh * _sigmoid(gh)))
    og = jnp.concatenate(parts, axis=1).astype(BF16)
    y_ref[...] = (x_ref[...] + _dot(og, wo_ref[0:GDN_WIDTH, :])
                  + _dot(oa_ref[...], wo_ref[GDN_WIDTH:GDN_WIDTH + DIFF_WIDTH, :]))


def _mix(x2d, of, ob, gate, oa, gn, wo):
    T = x2d.shape[0]
    tm = min(512, T)
    row = lambda i: (i, 0)
    fixed = lambda i: (0, 0)
    half = pl.BlockSpec((tm, GDN_WIDTH), row)
    return pl.pallas_call(
        _mix_kernel,
        grid=(T // tm,),
        in_specs=[pl.BlockSpec((tm, D_MODEL), row), half, half, half, half,
                  pl.BlockSpec((1, GDN_DV), fixed),
                  pl.BlockSpec((GDN_WIDTH + DIFF_WIDTH, D_MODEL), fixed)],
        out_specs=pl.BlockSpec((tm, D_MODEL), row),
        out_shape=jax.ShapeDtypeStruct((T, D_MODEL), F32),
        compiler_params=_params(("parallel",)),
        name="mix_out",
    )(x2d, of, ob, gate, oa, gn, wo)


def _mlp_kernel(final, x_ref, nw_ref, wu_ref, wd_ref, nf_ref, y_ref):
    x = x_ref[...]
    ms = jnp.mean(x * x, axis=-1, keepdims=True)
    h = (x * lax.rsqrt(ms + EPS) * nw_ref[...]).astype(BF16)
    acc = x
    step = D_MODEL
    for c in range(D_FF // step):
        up = _dot(h, wu_ref[:, c * step:(c + 1) * step])
        act = jnp.square(jnp.maximum(up, 0.0)).astype(BF16)
        acc = acc + _dot(act, wd_ref[c * step:(c + 1) * step, :])
    if final:
        ms2 = jnp.mean(acc * acc, axis=-1, keepdims=True)
        acc = acc * lax.rsqrt(ms2 + EPS) * nf_ref[...]
    y_ref[...] = acc


def _mlp(x2d, nw, wu, wd, nf, final):
    T = x2d.shape[0]
    tm = min(512, T)
    row = lambda i: (i, 0)
    fixed = lambda i: (0, 0)
    return pl.pallas_call(
        functools.partial(_mlp_kernel, final),
        grid=(T // tm,),
        in_specs=[pl.BlockSpec((tm, D_MODEL), row),
                  pl.BlockSpec((1, D_MODEL), fixed),
                  pl.BlockSpec((D_MODEL, D_FF), fixed),
                  pl.BlockSpec((D_FF, D_MODEL), fixed),
                  pl.BlockSpec((1, D_MODEL), fixed)],
        out_specs=pl.BlockSpec((tm, D_MODEL), row),
        out_shape=jax.ShapeDtypeStruct((T, D_MODEL), F32),
        compiler_params=_params(("parallel",)),
        name="mlp",
    )(x2d, nw, wu, wd, nf)


def _prepare_weights(norm_mix, w_in, conv_w, a_log, dt_bias, gdn_norm, lambda_q1, lambda_k1,
                     lambda_q2, lambda_k2, diff_norm, w_out, norm_mlp, w_up, w_down, norm_final):
    layers = []
    ngate = 2 * GDN_HEADS
    for l in range(DEPTH):
        wm = jnp.concatenate([w_in[l][:, :AB_LO], w_in[l][:, AB_HI:]], axis=1).astype(BF16)
        wab = jnp.pad(w_in[l][:, AB_LO:AB_HI], ((0, 0), (0, LANES - (AB_HI - AB_LO)))).astype(BF16)
        gp = jnp.stack([jnp.pad(a_log[l].reshape(ngate).astype(F32), (0, LANES - ngate)),
                        jnp.pad(dt_bias[l].reshape(ngate).astype(F32), (0, LANES - ngate))])
        lp = jnp.stack([lambda_q1[l], lambda_k1[l], lambda_q2[l], lambda_k2[l]]).astype(F32)
        layers.append(dict(
            nmix=norm_mix[l].reshape(1, D_MODEL), wm=wm, wab=wab,
            cw=conv_w[l], gp=gp, gn=gdn_norm[l].reshape(1, GDN_DV),
            lp=lp, dn=diff_norm[l].reshape(1, DIFF_DV),
            wo=w_out[l].astype(BF16), nmlp=norm_mlp[l].reshape(1, D_MODEL),
            wu=w_up[l].astype(BF16), wd=w_down[l].astype(BF16)))
    slopes = jnp.asarray([-(2.0 ** (-8.0 * (h + 1) / DIFF_HEADS)) for h in range(DIFF_HEADS)], F32)
    nslopes = jnp.broadcast_to(slopes[:, None, None], (DIFF_HEADS, SUBLANES, LANES))
    return layers, nslopes, norm_final.reshape(1, D_MODEL)


def _trunk(x, layers, nslopes, nf):
    B, S, _ = x.shape
    T = B * S
    x2d = x.reshape(T, D_MODEL)
    for l, p in enumerate(layers):
        lambda_init = 0.8 - 0.6 * math.exp(-0.3 * l)
        qkv, gate, ab, dq, dk, dv = _inproj(x2d, p["nmix"], p["wm"], p["wab"])
        r3 = lambda t: t.reshape(B, S, t.shape[-1])
        q, k, v, g = _gdn_prep(r3(qkv), r3(ab), p["cw"], p["gp"])
        u, w, qd, kd, at, gl = _gdn_intra(q, k, v, g)
        of, ob = _gdn_scan(u, w, qd, kd, at, gl)
        oa = _attention(r3(dq), r3(dk), r3(dv), nslopes, p["lp"], p["dn"], lambda_init)
        x2d = _mix(x2d, of.reshape(T, GDN_WIDTH), ob.reshape(T, GDN_WIDTH), gate,
                   oa.reshape(T, DIFF_WIDTH), p["gn"], p["wo"])
        x2d = _mlp(x2d, p["nmlp"], p["wu"], p["wd"], nf, l == DEPTH - 1)
    return x2d.reshape(B, S, D_MODEL)


def kernel(x_prompt, x_sample, norm_mix, w_in, conv_w, a_log, dt_bias, gdn_norm, lambda_q1,
           lambda_k1, lambda_q2, lambda_k2, diff_norm, w_out, norm_mlp, w_up, w_down, norm_final):
    layers, nslopes, nf = _prepare_weights(
        norm_mix, w_in, conv_w, a_log, dt_bias, gdn_norm, lambda_q1, lambda_k1,
        lambda_q2, lambda_k2, diff_norm, w_out, norm_mlp, w_up, w_down, norm_final)
    return (_trunk(x_prompt, layers, nslopes, nf), _trunk(x_sample, layers, nslopes, nf))
```

```python
import functools
import math

import jax
import jax.numpy as jnp
import numpy as np
from jax import lax
from jax.experimental import pallas as pl
from jax.experimental.pallas import tpu as pltpu

D_MODEL = 1024
DEPTH = 2
GDN_HEADS = 4
GDN_DK = 128
GDN_DV = 128
GDN_WIDTH = GDN_HEADS * GDN_DV
CONV_W = 5
CONV_PAD = CONV_W // 2
CHUNK = 64
DIFF_HEADS = 4
DIFF_DH = 64
DIFF_DV = 2 * DIFF_DH
DIFF_WIDTH = DIFF_HEADS * DIFF_DV
D_FF = 4 * D_MODEL
EPS = 1e-6
QKV_W = 3 * GDN_WIDTH
AB_LO = QKV_W + GDN_WIDTH
AB_HI = AB_LO + 4 * GDN_HEADS
LANES = 128
SUBLANES = 8
VMEM_LIMIT = 56 * 1024 * 1024
NEG_BIG = -1e30
LOG2E = math.log2(math.e)
Q_SCALE = DIFF_DH ** -0.5 * LOG2E

BF16 = jnp.bfloat16
F32 = jnp.float32


def _dot(a, b):
    return jnp.dot(a, b, preferred_element_type=F32)


def _dot_nt(a, b):
    return lax.dot_general(a, b, (((1,), (1,)), ((), ())), preferred_element_type=F32)


def _dot_tn(a, b):
    return lax.dot_general(a, b, (((0,), (0,)), ((), ())), preferred_element_type=F32)


def _sigmoid(x):
    return 1.0 / (1.0 + jnp.exp(-x))


def _params(sem):
    return pltpu.CompilerParams(dimension_semantics=sem, vmem_limit_bytes=VMEM_LIMIT)


def _inproj_kernel(x_ref, nw_ref, wm_ref, wab_ref, wqt_ref, wvt_ref,
                   qkv_ref, gate_ref, ab_ref, dqt_ref, dk_ref, dvt_ref):
    x = x_ref[...]
    ms = jnp.mean(x * x, axis=-1, keepdims=True)
    h = (x * lax.rsqrt(ms + EPS) * nw_ref[...]).astype(BF16)
    qkv_ref[...] = _dot(h, wm_ref[:, 0:QKV_W])
    gate_ref[...] = _dot(h, wm_ref[:, QKV_W:AB_LO])
    dk_ref[...] = _dot(h, wm_ref[:, AB_LO:AB_LO + DIFF_WIDTH]).astype(BF16)
    ab_ref[...] = _dot(h, wab_ref[...])
    dqt_ref[...] = (_dot_nt(wqt_ref[...], h) * Q_SCALE).astype(BF16)
    dvt_ref[...] = _dot_nt(wvt_ref[...], h).astype(BF16)


def _inproj(x2d, nw, wm, wab, wqt, wvt):
    T = x2d.shape[0]
    tm = min(512, T)
    row = lambda i: (i, 0)
    col = lambda i: (0, i)
    fixed = lambda i: (0, 0)
    nmain = wm.shape[1]
    return pl.pallas_call(
        _inproj_kernel,
        grid=(T // tm,),
        in_specs=[pl.BlockSpec((tm, D_MODEL), row),
                  pl.BlockSpec((1, D_MODEL), fixed),
                  pl.BlockSpec((D_MODEL, nmain), fixed),
                  pl.BlockSpec((D_MODEL, LANES), fixed),
                  pl.BlockSpec((DIFF_WIDTH, D_MODEL), fixed),
                  pl.BlockSpec((DIFF_WIDTH, D_MODEL), fixed)],
        out_specs=[pl.BlockSpec((tm, QKV_W), row),
                   pl.BlockSpec((tm, GDN_WIDTH), row),
                   pl.BlockSpec((tm, LANES), row),
                   pl.BlockSpec((DIFF_WIDTH, tm), col),
                   pl.BlockSpec((tm, DIFF_WIDTH), row),
                   pl.BlockSpec((DIFF_WIDTH, tm), col)],
        out_shape=[jax.ShapeDtypeStruct((T, QKV_W), F32),
                   jax.ShapeDtypeStruct((T, GDN_WIDTH), F32),
                   jax.ShapeDtypeStruct((T, LANES), F32),
                   jax.ShapeDtypeStruct((DIFF_WIDTH, T), BF16),
                   jax.ShapeDtypeStruct((T, DIFF_WIDTH), BF16),
                   jax.ShapeDtypeStruct((DIFF_WIDTH, T), BF16)],
        compiler_params=_params(("parallel",)),
        name="inproj",
    )(x2d, nw, wm, wab, wqt, wvt)


def _gdn_prep_kernel(cur_ref, prev_ref, next_ref, ab_ref, cw_ref, gp_ref,
                     q_ref, k_ref, v_ref, g_ref, xpad):
    i = pl.program_id(1)
    n = pl.num_programs(1)
    tm = cur_ref.shape[1]
    halo = SUBLANES
    xpad[halo:halo + tm, :] = cur_ref[0]
    xpad[0:halo, :] = jnp.where(i > 0, prev_ref[0], 0.0)
    xpad[halo + tm:2 * halo + tm, :] = jnp.where(i < n - 1, next_ref[0], 0.0)

    outs = (q_ref, k_ref, v_ref)
    for grp in range(3):
        c0 = grp * GDN_WIDTH
        acc = None
        for kk in range(CONV_W):
            r0 = halo - CONV_PAD + kk
            term = xpad[r0:r0 + tm, c0:c0 + GDN_WIDTH] * cw_ref[kk:kk + 1, c0:c0 + GDN_WIDTH]
            acc = term if acc is None else acc + term
        y = acc * _sigmoid(acc)
        if grp < 2:
            parts = []
            for h in range(GDN_HEADS):
                yh = y[:, h * GDN_DK:(h + 1) * GDN_DK]
                yh = yh * lax.rsqrt(jnp.sum(yh * yh, axis=-1, keepdims=True) + EPS)
                if grp == 0:
                    yh = yh * (GDN_DK ** -0.5)
                parts.append(yh)
            y = jnp.concatenate(parts, axis=1)
        outs[grp][0] = y

    ab = ab_ref[0]
    a_coef = -jnp.exp(gp_ref[0:1, :])
    z = ab + gp_ref[1:2, :]
    softplus = jnp.maximum(z, 0.0) + jnp.log1p(jnp.exp(-jnp.abs(z)))
    lane = lax.broadcasted_iota(jnp.int32, ab.shape, 1)
    ng = 2 * GDN_HEADS
    g_ref[0] = jnp.where(lane < ng, a_coef * softplus,
                         jnp.where(lane < 2 * ng, _sigmoid(ab), 0.0))


def _gdn_prep(qkv, ab, cw, gp):
    B, S, _ = qkv.shape
    tm = min(256, S)
    nb = tm // SUBLANES
    last = S // SUBLANES - 1
    tok = lambda b, i: (b, i, 0)
    fixed = lambda b, i: (0, 0)
    return pl.pallas_call(
        _gdn_prep_kernel,
        grid=(B, S // tm),
        in_specs=[pl.BlockSpec((1, tm, QKV_W), tok),
                  pl.BlockSpec((1, SUBLANES, QKV_W), lambda b, i: (b, jnp.maximum(i * nb - 1, 0), 0)),
                  pl.BlockSpec((1, SUBLANES, QKV_W), lambda b, i: (b, jnp.minimum((i + 1) * nb, last), 0)),
                  pl.BlockSpec((1, tm, LANES), tok),
                  pl.BlockSpec((CONV_W, QKV_W), fixed),
                  pl.BlockSpec((2, LANES), fixed)],
        out_specs=[pl.BlockSpec((1, tm, GDN_WIDTH), tok)] * 3 + [pl.BlockSpec((1, tm, LANES), tok)],
        out_shape=[jax.ShapeDtypeStruct((B, S, GDN_WIDTH), F32)] * 3
                  + [jax.ShapeDtypeStruct((B, S, LANES), F32)],
        scratch_shapes=[pltpu.VMEM((tm + 2 * SUBLANES, QKV_W), F32)],
        compiler_params=_params(("parallel", "parallel")),
        name="gdn_prep",
    )(qkv, qkv, qkv, ab, cw, gp)


def _gdn_intra_kernel(q_ref, k_ref, v_ref, g_ref,
                      u_ref, w_ref, qd_ref, kd_ref, at_ref, gl_ref):
    tc = q_ref.shape[1]
    C = CHUNK
    H = GDN_HEADS
    G = g_ref[0]
    pos = lax.broadcasted_iota(jnp.int32, G.shape, 0) & (C - 1)
    gcf = G
    gcb = G
    s = 1
    while s < C:
        gcf = gcf + jnp.where(pos >= s, pltpu.roll(gcf, s, axis=0), 0.0)
        gcb = gcb + jnp.where(pos < C - s, pltpu.roll(gcb, tc - s, axis=0), 0.0)
        s *= 2

    ri = lax.broadcasted_iota(jnp.int32, (C, C), 0)
    ci = lax.broadcasted_iota(jnp.int32, (C, C), 1)
    eye = (ri == ci).astype(F32)
    masks = ((ci <= ri, ci < ri, ri <= ci), (ci >= ri, ci > ri, ri >= ci))

    heads = [(c, h) for c in range(tc // C) for h in range(H)]
    units = [(c, h, d) for (c, h) in heads for d in range(2)]
    qs, ks, kkqk = {}, {}, {}
    for (c, h) in heads:
        r0 = c * C
        qs[c, h] = q_ref[0, r0:r0 + C, h * GDN_DK:(h + 1) * GDN_DK]
        ks[c, h] = k_ref[0, r0:r0 + C, h * GDN_DK:(h + 1) * GDN_DK]
    for (c, h) in heads:
        kq = jnp.concatenate([ks[c, h], qs[c, h]], axis=0).astype(BF16)
        kkqk[c, h] = _dot_nt(kq, ks[c, h].astype(BF16))

    pmat, apow, attn, rhs = {}, {}, {}, {}
    for (c, h, d) in units:
        r0 = c * C
        ln = d * H + h
        incl, strict, colsum_mask = masks[d]
        gc_all = gcf if d == 0 else gcb
        g_col = G[r0:r0 + C, ln:ln + 1]
        gc_col = gc_all[r0:r0 + C, ln:ln + 1]
        beta_col = G[r0:r0 + C, 2 * H + ln:2 * H + ln + 1]
        last = C - 1 if d == 0 else 0
        gc_last = gc_all[r0 + last:r0 + last + 1, ln:ln + 1]
        gc_row = jnp.sum(jnp.where(colsum_mask, g_col, 0.0), axis=0, keepdims=True)
        decay = jnp.exp(jnp.where(incl, gc_col - gc_row, NEG_BIG))
        e_col = jnp.exp(gc_col)
        m1 = kkqk[c, h]
        a = jnp.where(strict, (m1[:C] * beta_col) * decay, 0.0)
        attn[c, h, d] = (m1[C:] * decay).astype(BF16)
        pmat[c, h, d] = eye - a
        apow[c, h, d] = a.astype(BF16)
        kh = ks[c, h]
        vh = v_ref[0, r0:r0 + C, h * GDN_DV:(h + 1) * GDN_DV]
        rhs[c, h, d] = jnp.concatenate([vh * beta_col, kh * (beta_col * e_col)], axis=1).astype(BF16)
        lo = ln * GDN_DK
        qd_ref[0, r0:r0 + C, lo:lo + GDN_DK] = (qs[c, h] * e_col).astype(BF16)
        kd_ref[0, r0:r0 + C, lo:lo + GDN_DK] = (kh * jnp.exp(gc_last - gc_col)).astype(BF16)
        gl_ref[0, c, ln:ln + 1, :] = jnp.broadcast_to(jnp.exp(gc_last), (1, LANES))

    n = 2
    while n < C:
        for un in units:
            apow[un] = _dot(apow[un], apow[un]).astype(BF16)
        for un in units:
            pmat[un] = pmat[un] + _dot(pmat[un].astype(BF16), apow[un])
        n *= 2

    sols = {un: _dot(pmat[un].astype(BF16), rhs[un]) for un in units}
    for (c, h, d) in units:
        r0 = c * C
        ln = d * H + h
        lo = ln * GDN_DK
        sol = sols[c, h, d]
        u_ref[0, r0:r0 + C, lo:lo + GDN_DV] = sol[:, :GDN_DV]
        w_ref[0, r0:r0 + C, lo:lo + GDN_DK] = sol[:, GDN_DV:].astype(BF16)
        at_ref[0, r0:r0 + C, ln * C:(ln + 1) * C] = attn[c, h, d]


def _gdn_intra(q, k, v, g):
    B, S, _ = q.shape
    tc = min(128, S)
    tok = lambda b, i: (b, i, 0)
    wide = 2 * GDN_WIDTH
    return pl.pallas_call(
        _gdn_intra_kernel,
        grid=(B, S // tc),
        in_specs=[pl.BlockSpec((1, tc, GDN_WIDTH), tok)] * 3 + [pl.BlockSpec((1, tc, LANES), tok)],
        out_specs=[pl.BlockSpec((1, tc, wide), tok)] * 4
                  + [pl.BlockSpec((1, tc, 2 * GDN_HEADS * CHUNK), tok),
                     pl.BlockSpec((1, tc // CHUNK, 2 * GDN_HEADS, LANES), lambda b, i: (b, i, 0, 0))],
        out_shape=[jax.ShapeDtypeStruct((B, S, wide), F32)]
                  + [jax.ShapeDtypeStruct((B, S, wide), BF16)] * 3
                  + [jax.ShapeDtypeStruct((B, S, 2 * GDN_HEADS * CHUNK), BF16),
                     jax.ShapeDtypeStruct((B, S // CHUNK, 2 * GDN_HEADS, LANES), F32)],
        compiler_params=_params(("parallel", "parallel")),
        name="gdn_intra",
    )(q, k, v, g)


def _gdn_scan_kernel(uf_ref, ub_ref, wf_ref, wb_ref, qf_ref, qb_ref, kf_ref, kb_ref,
                     af_ref, ab_ref, gf_ref, gb_ref, of_ref, ob_ref, st_ref):
    n = pl.program_id(1)
    tb = uf_ref.shape[1]
    C = CHUNK
    H = GDN_HEADS
    nc = tb // C

    @pl.when(n == 0)
    def _():
        st_ref[...] = jnp.zeros_like(st_ref)

    refs = ((uf_ref, wf_ref, qf_ref, kf_ref, af_ref, gf_ref, of_ref),
            (ub_ref, wb_ref, qb_ref, kb_ref, ab_ref, gb_ref, ob_ref))
    chains = [(d, h) for d in range(2) for h in range(H)]
    for cc in range(nc):
        states, ms, vbs = {}, {}, {}
        for (d, h) in chains:
            u_ref, w_ref, qd_ref, kd_ref, at_ref, gl_ref, o_ref = refs[d]
            r0 = (cc if d == 0 else nc - 1 - cc) * C
            cols = slice(h * GDN_DK, (h + 1) * GDN_DK)
            states[d, h] = st_ref[d * H + h]
            wq = jnp.concatenate([w_ref[0, r0:r0 + C, cols], qd_ref[0, r0:r0 + C, cols]], axis=0)
            ms[d, h] = _dot(wq, states[d, h].astype(BF16))
        for (d, h) in chains:
            u_ref = refs[d][0]
            r0 = (cc if d == 0 else nc - 1 - cc) * C
            cols = slice(h * GDN_DK, (h + 1) * GDN_DK)
            vbs[d, h] = (u_ref[0, r0:r0 + C, cols] - ms[d, h][:C]).astype(BF16)
        for (d, h) in chains:
            u_ref, w_ref, qd_ref, kd_ref, at_ref, gl_ref, o_ref = refs[d]
            c = cc if d == 0 else nc - 1 - cc
            r0 = c * C
            ln = d * H + h
            cols = slice(h * GDN_DK, (h + 1) * GDN_DK)
            o_ref[0, r0:r0 + C, cols] = ms[d, h][C:] + _dot(at_ref[0, r0:r0 + C, h * C:(h + 1) * C], vbs[d, h])
            gl = gl_ref[0, c, ln:ln + 1, :]
            st_ref[ln] = states[d, h] * gl + _dot_tn(kd_ref[0, r0:r0 + C, cols], vbs[d, h])


def _gdn_scan(u, w, qd, kd, at, gl):
    B, S, _ = u.shape
    tb = min(256, S)
    nb = S // tb
    nc = tb // CHUNK
    fwd = lambda b, n: (b, n, 0)
    bwd = lambda b, n: (b, nb - 1 - n, 1)
    fwd4 = lambda b, n: (b, n, 0, 0)
    bwd4 = lambda b, n: (b, nb - 1 - n, 0, 0)
    wide = GDN_WIDTH
    aw = GDN_HEADS * CHUNK
    pair = lambda shape: [pl.BlockSpec(shape, fwd), pl.BlockSpec(shape, bwd)]
    glshape = (1, nc, 2 * GDN_HEADS, LANES)
    return pl.pallas_call(
        _gdn_scan_kernel,
        grid=(B, nb),
        in_specs=pair((1, tb, wide)) * 4 + pair((1, tb, aw))
                 + [pl.BlockSpec(glshape, fwd4), pl.BlockSpec(glshape, bwd4)],
        out_specs=[pl.BlockSpec((1, tb, wide), fwd),
                   pl.BlockSpec((1, tb, wide), lambda b, n: (b, nb - 1 - n, 0))],
        out_shape=[jax.ShapeDtypeStruct((B, S, wide), F32)] * 2,
        scratch_shapes=[pltpu.VMEM((2 * GDN_HEADS, GDN_DK, GDN_DV), F32)],
        compiler_params=_params(("parallel", "arbitrary")),
        name="gdn_scan",
    )(u, u, w, w, qd, qd, kd, kd, at, at, gl, gl)


def _attn_kernel(lambda_init, tk, qt_ref, k_ref, vt_ref, ct_ref, lp_ref, dn_ref, o_ref,
                 w_scr, pk_scr, sa_scr, sb_scr, pa_scr, pb_scr, acc_scr):
    i = pl.program_id(2)
    tq = qt_ref.shape[1]
    S = k_ref.shape[1]
    nk = S // tk
    half = tk // 2
    wide = 2 * tq

    qt = qt_ref[...]
    row = lax.broadcasted_iota(jnp.int32, qt.shape, 0)
    zero = jnp.zeros_like(qt)
    top = jnp.concatenate([jnp.where(row < DIFF_DH, qt, zero),
                           jnp.where(row >= DIFF_DH, qt, zero)], axis=1)
    ct = ct_ref[0]
    rep = wide // LANES
    rw = lax.broadcasted_iota(jnp.int32, (LANES, wide), 0)
    pos = jnp.where(rw == 0, jnp.tile(ct[0:1], (1, rep)),
                    jnp.where(rw == 1, jnp.tile(ct[1:2], (1, rep)),
                              jnp.where(rw == 2, jnp.tile(ct[2:3], (1, rep)), 0.0)))
    for kind, sign in enumerate((1.0, 0.0, -1.0)):
        w_scr[kind, 0:LANES, :] = top
        w_scr[kind, LANES:2 * LANES, :] = (sign * pos).astype(BF16)
    kr = lax.broadcasted_iota(jnp.int32, (tk, LANES), 0)
    kl = lax.broadcasted_iota(jnp.int32, (tk, LANES), 1)
    pk_scr[...] = jnp.where(kl < 3, (kr - half).astype(F32), 0.0).astype(BF16)
    acc_scr[...] = jnp.zeros_like(acc_scr)
    pb_scr[...] = jnp.zeros_like(pb_scr)
    c = ct[3:4, 0:1]
    li = lax.broadcasted_iota(jnp.int32, (1, wide), 1)
    tpos = (jnp.where(li >= tq, li - tq, li) + i * tq).astype(F32)
    jd = (i * tq) // tk

    def scores(j):
        j0 = pl.multiple_of(j * tk, tk)
        kind = jnp.where(j < jd, 0, jnp.where(j == jd, 1, 2))
        ka = jnp.concatenate([k_ref[0, pl.ds(j0, tk), :], pk_scr[...]], axis=1)
        return _dot(ka, w_scr[kind])

    def pv(j, p_ref):
        j0 = pl.multiple_of(j * tk, tk)
        return _dot(vt_ref[:, pl.ds(j0, tk)], p_ref[...])

    def half_step(j, s_cur, s_nxt, p_cur, p_prev, carry):
        m_run, l_run, alpha_prev = carry

        @pl.when(j == jd)
        def _():
            kpos = (j * tk + lax.broadcasted_iota(jnp.int32, (tk, tq), 0)).astype(F32)
            qpos = (i * tq + lax.broadcasted_iota(jnp.int32, (tk, tq), 1)).astype(F32)
            bias = -c * jnp.abs(qpos - kpos)
            s_cur[...] = s_cur[...] + jnp.concatenate([bias, bias], axis=1)

        s_nxt[...] = scores(jnp.minimum(j + 1, nk - 1))
        sigma = jnp.where(j < jd, -1.0, jnp.where(j == jd, 0.0, 1.0)).astype(F32)
        su = (sigma * c) * (tpos - (j * tk + half).astype(F32))
        s = s_cur[...]
        m_new = jnp.maximum(m_run, jnp.max(s, axis=0, keepdims=True) + su)
        p = jnp.exp2(s - (m_new - su))
        alpha = jnp.exp2(m_run - m_new)
        l_new = alpha * l_run + jnp.sum(p, axis=0, keepdims=True)
        p_cur[...] = p.astype(BF16)
        acc_scr[...] = acc_scr[...] * alpha_prev + pv(jnp.maximum(j - 1, 0), p_prev)
        return m_new, l_new, alpha

    def pair(jj, carry):
        carry = half_step(2 * jj, sa_scr, sb_scr, pa_scr, pb_scr, carry)
        return half_step(2 * jj + 1, sb_scr, sa_scr, pb_scr, pa_scr, carry)

    sa_scr[...] = scores(0)
    init = (jnp.full((1, wide), NEG_BIG, F32), jnp.zeros((1, wide), F32), jnp.ones((1, wide), F32))
    _, l_run, alpha_last = lax.fori_loop(0, nk // 2, pair, init)
    acc_scr[...] = acc_scr[...] * alpha_last + pv(nk - 1, pb_scr)

    lp = lp_ref[...]
    lam = (jnp.exp(jnp.sum(lp[0:1] * lp[1:2], axis=1, keepdims=True))
           - jnp.exp(jnp.sum(lp[2:3] * lp[3:4], axis=1, keepdims=True)) + lambda_init)
    o_all = acc_scr[...] / l_run
    o = o_all[:, :tq] - lam * o_all[:, tq:]
    o = o * lax.rsqrt(jnp.mean(o * o, axis=0, keepdims=True) + EPS) * jnp.tile(dn_ref[...], (1, tq // LANES))
    o_ref[0] = (o * (1.0 - lambda_init)).T.astype(o_ref.dtype)


def _attention(dqt, dk, dvt, ctab, lp, dn, lambda_init):
    B, S, _ = dk.shape
    tk = min(512, S // 2)
    tq = min(256, tk)
    assert tk % tq == 0 and tk // 2 <= 256 and (S // tk) % 2 == 0
    nq = S // tq
    return pl.pallas_call(
        functools.partial(_attn_kernel, lambda_init, tk),
        grid=(B, DIFF_HEADS, nq),
        in_specs=[pl.BlockSpec((DIFF_DV, tq), lambda b, h, i: (h, b * nq + i)),
                  pl.BlockSpec((1, S, DIFF_DV), lambda b, h, i: (b, 0, h)),
                  pl.BlockSpec((DIFF_DV, S), lambda b, h, i: (h, b)),
                  pl.BlockSpec((1, SUBLANES, LANES), lambda b, h, i: (h, 0, 0)),
                  pl.BlockSpec((4, DIFF_DH), lambda b, h, i: (0, 0)),
                  pl.BlockSpec((DIFF_DV, LANES), lambda b, h, i: (0, 0))],
        out_specs=pl.BlockSpec((1, tq, DIFF_DV), lambda b, h, i: (b, i, h)),
        out_shape=jax.ShapeDtypeStruct((B, S, DIFF_WIDTH), BF16),
        scratch_shapes=[pltpu.VMEM((3, 2 * LANES, 2 * tq), BF16),
                        pltpu.VMEM((tk, LANES), BF16),
                        pltpu.VMEM((tk, 2 * tq), F32),
                        pltpu.VMEM((tk, 2 * tq), F32),
                        pltpu.VMEM((tk, 2 * tq), BF16),
                        pltpu.VMEM((tk, 2 * tq), BF16),
                        pltpu.VMEM((DIFF_DV, 2 * tq), F32)],
        compiler_params=_params(("parallel", "parallel", "parallel")),
        name="diff_attn",
    )(dqt, dk, dvt, ctab, lp, dn)


def _mix_kernel(x_ref, of_ref, ob_ref, gate_ref, oa_ref, gn_ref, wo_ref, y_ref):
    o = of_ref[...] + ob_ref[...]
    gate = gate_ref[...]
    parts = []
    for h in range(GDN_HEADS):
        cols = slice(h * GDN_DV, (h + 1) * GDN_DV)
        oh = o[:, cols]
        oh = oh * lax.rsqrt(jnp.mean(oh * oh, axis=-1, keepdims=True) + EPS) * gn_ref[...]
        gh = gate[:, cols]
        parts.append(oh * (gh * _sigmoid(gh)))
    og = jnp.concatenate(parts, axis=1).astype(BF16)
    y_ref[...] = (x_ref[...] + _dot(og, wo_ref[0:GDN_WIDTH, :])
                  + _dot(oa_ref[...], wo_ref[GDN_WIDTH:GDN_WIDTH + DIFF_WIDTH, :]))


def _mix(x2d, of, ob, gate, oa, gn, wo):
    T = x2d.shape[0]
    tm = min(512, T)
    row = lambda i: (i, 0)
    fixed = lambda i: (0, 0)
    half = pl.BlockSpec((tm, GDN_WIDTH), row)
    return pl.pallas_call(
        _mix_kernel,
        grid=(T // tm,),
        in_specs=[pl.BlockSpec((tm, D_MODEL), row), half, half, half, half,
                  pl.BlockSpec((1, GDN_DV), fixed),
                  pl.BlockSpec((GDN_WIDTH + DIFF_WIDTH, D_MODEL), fixed)],
        out_specs=pl.BlockSpec((tm, D_MODEL), row),
        out_shape=jax.ShapeDtypeStruct((T, D_MODEL), F32),
        compiler_params=_params(("parallel",)),
        name="mix_out",
    )(x2d, of, ob, gate, oa, gn, wo)


def _mlp_kernel(final, x_ref, nw_ref, wu_ref, wd_ref, nf_ref, y_ref):
    x = x_ref[...]
    ms = jnp.mean(x * x, axis=-1, keepdims=True)
    h = (x * lax.rsqrt(ms + EPS) * nw_ref[...]).astype(BF16)
    acc = x
    step = D_MODEL
    for c in range(D_FF // step):
        up = _dot(h, wu_ref[:, c * step:(c + 1) * step])
        act = jnp.square(jnp.maximum(up, 0.0)).astype(BF16)
        acc = acc + _dot(act, wd_ref[c * step:(c + 1) * step, :])
    if final:
        ms2 = jnp.mean(acc * acc, axis=-1, keepdims=True)
        acc = acc * lax.rsqrt(ms2 + EPS) * nf_ref[...]
    y_ref[...] = acc


def _mlp(x2d, nw, wu, wd, nf, final):
    T = x2d.shape[0]
    tm = min(512, T)
    row = lambda i: (i, 0)
    fixed = lambda i: (0, 0)
    return pl.pallas_call(
        functools.partial(_mlp_kernel, final),
        grid=(T // tm,),
        in_specs=[pl.BlockSpec((tm, D_MODEL), row),
                  pl.BlockSpec((1, D_MODEL), fixed),
                  pl.BlockSpec((D_MODEL, D_FF), fixed),
                  pl.BlockSpec((D_FF, D_MODEL), fixed),
                  pl.BlockSpec((1, D_MODEL), fixed)],
        out_specs=pl.BlockSpec((tm, D_MODEL), row),
        out_shape=jax.ShapeDtypeStruct((T, D_MODEL), F32),
        compiler_params=_params(("parallel",)),
        name="mlp",
    )(x2d, nw, wu, wd, nf)


def _prepare_weights(norm_mix, w_in, conv_w, a_log, dt_bias, gdn_norm, lambda_q1, lambda_k1,
                     lambda_q2, lambda_k2, diff_norm, w_out, norm_mlp, w_up, w_down, norm_final):
    layers = []
    ngate = 2 * GDN_HEADS
    q_lo, k_lo, v_lo = AB_HI, AB_HI + DIFF_WIDTH, AB_HI + 2 * DIFF_WIDTH
    for l in range(DEPTH):
        wm = jnp.concatenate([w_in[l][:, :AB_LO], w_in[l][:, k_lo:v_lo]], axis=1).astype(BF16)
        wab = jnp.pad(w_in[l][:, AB_LO:AB_HI], ((0, 0), (0, LANES - (AB_HI - AB_LO)))).astype(BF16)
        wqt = w_in[l][:, q_lo:k_lo].T.astype(BF16)
        wvt = w_in[l][:, v_lo:v_lo + DIFF_WIDTH].T.astype(BF16)
        gp = jnp.stack([jnp.pad(a_log[l].reshape(ngate).astype(F32), (0, LANES - ngate)),
                        jnp.pad(dt_bias[l].reshape(ngate).astype(F32), (0, LANES - ngate))])
        lp = jnp.stack([lambda_q1[l], lambda_k1[l], lambda_q2[l], lambda_k2[l]]).astype(F32)
        layers.append(dict(
            nmix=norm_mix[l].reshape(1, D_MODEL), wm=wm, wab=wab, wqt=wqt, wvt=wvt,
            cw=conv_w[l], gp=gp, gn=gdn_norm[l].reshape(1, GDN_DV),
            lp=lp, dn=jnp.broadcast_to(diff_norm[l].astype(F32)[:, None], (DIFF_DV, LANES)),
            wo=w_out[l].astype(BF16), nmlp=norm_mlp[l].reshape(1, D_MODEL),
            wu=w_up[l].astype(BF16), wd=w_down[l].astype(BF16)))
    return layers, jnp.asarray(_alibi_table()), norm_final.reshape(1, D_MODEL)


def _bf16_round(x):
    u = np.asarray(x, np.float32).view(np.uint32)
    u = (u + np.uint32(0x7FFF) + ((u >> np.uint32(16)) & np.uint32(1))) & np.uint32(0xFFFF0000)
    return u.view(np.float32)


def _alibi_table():
    tab = np.zeros((DIFF_HEADS, SUBLANES, LANES), np.float32)
    for h in range(DIFF_HEADS):
        c = np.float32(2.0 ** (-8.0 * (h + 1) / DIFF_HEADS) * LOG2E)
        hi = _bf16_round(c)
        mid = _bf16_round(np.float32(c - hi))
        lo = _bf16_round(np.float32(c - hi - mid))
        tab[h, 0], tab[h, 1], tab[h, 2] = hi, mid, lo
        tab[h, 3] = np.float32(np.float32(hi + mid) + lo)
    return tab


def _trunk(x, layers, ctab, nf):
    B, S, _ = x.shape
    T = B * S
    x2d = x.reshape(T, D_MODEL)
    for l, p in enumerate(layers):
        lambda_init = 0.8 - 0.6 * math.exp(-0.3 * l)
        qkv, gate, ab, dqt, dk, dvt = _inproj(x2d, p["nmix"], p["wm"], p["wab"], p["wqt"], p["wvt"])
        r3 = lambda t: t.reshape(B, S, t.shape[-1])
        q, k, v, g = _gdn_prep(r3(qkv), r3(ab), p["cw"], p["gp"])
        u, w, qd, kd, at, gl = _gdn_intra(q, k, v, g)
        of, ob = _gdn_scan(u, w, qd, kd, at, gl)
        oa = _attention(dqt, r3(dk), dvt, ctab, p["lp"], p["dn"], lambda_init)
        x2d = _mix(x2d, of.reshape(T, GDN_WIDTH), ob.reshape(T, GDN_WIDTH), gate,
                   oa.reshape(T, DIFF_WIDTH), p["gn"], p["wo"])
        x2d = _mlp(x2d, p["nmlp"], p["wu"], p["wd"], nf, l == DEPTH - 1)
    return x2d.reshape(B, S, D_MODEL)


def kernel(x_prompt, x_sample, norm_mix, w_in, conv_w, a_log, dt_bias, gdn_norm, lambda_q1,
           lambda_k1, lambda_q2, lambda_k2, diff_norm, w_out, norm_mlp, w_up, w_down, norm_final):
    layers, ctab, nf = _prepare_weights(
        norm_mix, w_in, conv_w, a_log, dt_bias, gdn_norm, lambda_q1, lambda_k1,
        lambda_q2, lambda_k2, diff_norm, w_out, norm_mlp, w_up, w_down, norm_final)
    return (_trunk(x_prompt, layers, ctab, nf), _trunk(x_sample, layers, ctab, nf))
```

```python
import functools
import math

import jax
import jax.numpy as jnp
import numpy as np
from jax import lax
from jax.experimental import pallas as pl
from jax.experimental.pallas import tpu as pltpu

D_MODEL = 1024
DEPTH = 2
GDN_HEADS = 4
GDN_DK = 128
GDN_DV = 128
GDN_WIDTH = GDN_HEADS * GDN_DV
CONV_W = 5
CONV_PAD = CONV_W // 2
CHUNK = 64
DIFF_HEADS = 4
DIFF_DH = 64
DIFF_DV = 2 * DIFF_DH
DIFF_WIDTH = DIFF_HEADS * DIFF_DV
D_FF = 4 * D_MODEL
EPS = 1e-6
QKV_W = 3 * GDN_WIDTH
AB_LO = QKV_W + GDN_WIDTH
AB_HI = AB_LO + 4 * GDN_HEADS
LANES = 128
SUBLANES = 8
VMEM_LIMIT = 56 * 1024 * 1024
NEG_BIG = -1e30
LOG2E = math.log2(math.e)
Q_SCALE = DIFF_DH ** -0.5 * LOG2E

BF16 = jnp.bfloat16
F32 = jnp.float32


def _dot(a, b):
    return jnp.dot(a, b, preferred_element_type=F32)


def _dot_nt(a, b):
    return lax.dot_general(a, b, (((1,), (1,)), ((), ())), preferred_element_type=F32)


def _dot_tn(a, b):
    return lax.dot_general(a, b, (((0,), (0,)), ((), ())), preferred_element_type=F32)


def _sigmoid(x):
    return 1.0 / (1.0 + jnp.exp(-x))


def _params(sem):
    return pltpu.CompilerParams(dimension_semantics=sem, vmem_limit_bytes=VMEM_LIMIT)


def _inproj_kernel(x_ref, nw_ref, wm_ref, wab_ref, wqt_ref, wvt_ref,
                   qkv_ref, gate_ref, ab_ref, dqt_ref, dk_ref, dvt_ref):
    x = x_ref[...]
    ms = jnp.mean(x * x, axis=-1, keepdims=True)
    h = (x * lax.rsqrt(ms + EPS) * nw_ref[...]).astype(BF16)
    qkv_ref[...] = _dot(h, wm_ref[:, 0:QKV_W])
    gate_ref[...] = _dot(h, wm_ref[:, QKV_W:AB_LO])
    dk_ref[...] = _dot(h, wm_ref[:, AB_LO:AB_LO + DIFF_WIDTH]).astype(BF16)
    ab_ref[...] = _dot(h, wab_ref[...])
    dqt_ref[...] = (_dot_nt(wqt_ref[...], h) * Q_SCALE).astype(BF16)
    dvt_ref[...] = _dot_nt(wvt_ref[...], h).astype(BF16)


def _inproj(x2d, nw, wm, wab, wqt, wvt):
    T = x2d.shape[0]
    tm = min(512, T)
    row = lambda i: (i, 0)
    col = lambda i: (0, i)
    fixed = lambda i: (0, 0)
    nmain = wm.shape[1]
    return pl.pallas_call(
        _inproj_kernel,
        grid=(T // tm,),
        in_specs=[pl.BlockSpec((tm, D_MODEL), row),
                  pl.BlockSpec((1, D_MODEL), fixed),
                  pl.BlockSpec((D_MODEL, nmain), fixed),
                  pl.BlockSpec((D_MODEL, LANES), fixed),
                  pl.BlockSpec((DIFF_WIDTH, D_MODEL), fixed),
                  pl.BlockSpec((DIFF_WIDTH, D_MODEL), fixed)],
        out_specs=[pl.BlockSpec((tm, QKV_W), row),
                   pl.BlockSpec((tm, GDN_WIDTH), row),
                   pl.BlockSpec((tm, LANES), row),
                   pl.BlockSpec((DIFF_WIDTH, tm), col),
                   pl.BlockSpec((tm, DIFF_WIDTH), row),
                   pl.BlockSpec((DIFF_WIDTH, tm), col)],
        out_shape=[jax.ShapeDtypeStruct((T, QKV_W), F32),
                   jax.ShapeDtypeStruct((T, GDN_WIDTH), F32),
                   jax.ShapeDtypeStruct((T, LANES), F32),
                   jax.ShapeDtypeStruct((DIFF_WIDTH, T), BF16),
                   jax.ShapeDtypeStruct((T, DIFF_WIDTH), BF16),
                   jax.ShapeDtypeStruct((DIFF_WIDTH, T), BF16)],
        compiler_params=_params(("parallel",)),
        name="inproj",
    )(x2d, nw, wm, wab, wqt, wvt)


def _gdn_prep_kernel(cur_ref, prev_ref, next_ref, ab_ref, cw_ref, gp_ref,
                     q_ref, k_ref, v_ref, g_ref, xpad):
    i = pl.program_id(1)
    n = pl.num_programs(1)
    tm = cur_ref.shape[1]
    halo = SUBLANES
    xpad[halo:halo + tm, :] = cur_ref[0]
    xpad[0:halo, :] = jnp.where(i > 0, prev_ref[0], 0.0)
    xpad[halo + tm:2 * halo + tm, :] = jnp.where(i < n - 1, next_ref[0], 0.0)

    outs = (q_ref, k_ref, v_ref)
    for grp in range(3):
        c0 = grp * GDN_WIDTH
        acc = None
        for kk in range(CONV_W):
            r0 = halo - CONV_PAD + kk
            term = xpad[r0:r0 + tm, c0:c0 + GDN_WIDTH] * cw_ref[kk:kk + 1, c0:c0 + GDN_WIDTH]
            acc = term if acc is None else acc + term
        y = acc * _sigmoid(acc)
        if grp < 2:
            parts = []
            for h in range(GDN_HEADS):
                yh = y[:, h * GDN_DK:(h + 1) * GDN_DK]
                yh = yh * lax.rsqrt(jnp.sum(yh * yh, axis=-1, keepdims=True) + EPS)
                if grp == 0:
                    yh = yh * (GDN_DK ** -0.5)
                parts.append(yh)
            y = jnp.concatenate(parts, axis=1)
        outs[grp][0] = y

    ab = ab_ref[0]
    a_coef = -jnp.exp(gp_ref[0:1, :])
    z = ab + gp_ref[1:2, :]
    softplus = jnp.maximum(z, 0.0) + jnp.log1p(jnp.exp(-jnp.abs(z)))
    lane = lax.broadcasted_iota(jnp.int32, ab.shape, 1)
    ng = 2 * GDN_HEADS
    g_ref[0] = jnp.where(lane < ng, a_coef * softplus,
                         jnp.where(lane < 2 * ng, _sigmoid(ab), 0.0))


def _gdn_prep(qkv, ab, cw, gp):
    B, S, _ = qkv.shape
    tm = min(256, S)
    nb = tm // SUBLANES
    last = S // SUBLANES - 1
    tok = lambda b, i: (b, i, 0)
    fixed = lambda b, i: (0, 0)
    return pl.pallas_call(
        _gdn_prep_kernel,
        grid=(B, S // tm),
        in_specs=[pl.BlockSpec((1, tm, QKV_W), tok),
                  pl.BlockSpec((1, SUBLANES, QKV_W), lambda b, i: (b, jnp.maximum(i * nb - 1, 0), 0)),
                  pl.BlockSpec((1, SUBLANES, QKV_W), lambda b, i: (b, jnp.minimum((i + 1) * nb, last), 0)),
                  pl.BlockSpec((1, tm, LANES), tok),
                  pl.BlockSpec((CONV_W, QKV_W), fixed),
                  pl.BlockSpec((2, LANES), fixed)],
        out_specs=[pl.BlockSpec((1, tm, GDN_WIDTH), tok)] * 3 + [pl.BlockSpec((1, tm, LANES), tok)],
        out_shape=[jax.ShapeDtypeStruct((B, S, GDN_WIDTH), F32)] * 3
                  + [jax.ShapeDtypeStruct((B, S, LANES), F32)],
        scratch_shapes=[pltpu.VMEM((tm + 2 * SUBLANES, QKV_W), F32)],
        compiler_params=_params(("parallel", "parallel")),
        name="gdn_prep",
    )(qkv, qkv, qkv, ab, cw, gp)


def _gdn_intra_kernel(q_ref, k_ref, v_ref, g_ref,
                      u_ref, w_ref, qd_ref, kd_ref, at_ref, gl_ref):
    tc = q_ref.shape[1]
    C = CHUNK
    H = GDN_HEADS
    G = g_ref[0]
    pos = lax.broadcasted_iota(jnp.int32, G.shape, 0) & (C - 1)
    gcf = G
    gcb = G
    s = 1
    while s < C:
        gcf = gcf + jnp.where(pos >= s, pltpu.roll(gcf, s, axis=0), 0.0)
        gcb = gcb + jnp.where(pos < C - s, pltpu.roll(gcb, tc - s, axis=0), 0.0)
        s *= 2

    ri = lax.broadcasted_iota(jnp.int32, (C, C), 0)
    ci = lax.broadcasted_iota(jnp.int32, (C, C), 1)
    eye = (ri == ci).astype(F32)
    masks = ((ci <= ri, ci < ri, ri <= ci), (ci >= ri, ci > ri, ri >= ci))

    heads = [(c, h) for c in range(tc // C) for h in range(H)]
    units = [(c, h, d) for (c, h) in heads for d in range(2)]
    qs, ks, kkqk = {}, {}, {}
    for (c, h) in heads:
        r0 = c * C
        qs[c, h] = q_ref[0, r0:r0 + C, h * GDN_DK:(h + 1) * GDN_DK]
        ks[c, h] = k_ref[0, r0:r0 + C, h * GDN_DK:(h + 1) * GDN_DK]
    for (c, h) in heads:
        kq = jnp.concatenate([ks[c, h], qs[c, h]], axis=0).astype(BF16)
        kkqk[c, h] = _dot_nt(kq, ks[c, h].astype(BF16))

    pmat, apow, attn, rhs = {}, {}, {}, {}
    for (c, h, d) in units:
        r0 = c * C
        ln = d * H + h
        incl, strict, colsum_mask = masks[d]
        gc_all = gcf if d == 0 else gcb
        g_col = G[r0:r0 + C, ln:ln + 1]
        gc_col = gc_all[r0:r0 + C, ln:ln + 1]
        beta_col = G[r0:r0 + C, 2 * H + ln:2 * H + ln + 1]
        last = C - 1 if d == 0 else 0
        gc_last = gc_all[r0 + last:r0 + last + 1, ln:ln + 1]
        gc_row = jnp.sum(jnp.where(colsum_mask, g_col, 0.0), axis=0, keepdims=True)
        decay = jnp.exp(jnp.where(incl, gc_col - gc_row, NEG_BIG))
        e_col = jnp.exp(gc_col)
        m1 = kkqk[c, h]
        a = jnp.where(strict, (m1[:C] * beta_col) * decay, 0.0)
        attn[c, h, d] = (m1[C:] * decay).astype(BF16)
        pmat[c, h, d] = eye - a
        apow[c, h, d] = a.astype(BF16)
        kh = ks[c, h]
        vh = v_ref[0, r0:r0 + C, h * GDN_DV:(h + 1) * GDN_DV]
        rhs[c, h, d] = jnp.concatenate([vh * beta_col, kh * (beta_col * e_col)], axis=1).astype(BF16)
        lo = ln * GDN_DK
        qd_ref[0, r0:r0 + C, lo:lo + GDN_DK] = (qs[c, h] * e_col).astype(BF16)
        kd_ref[0, r0:r0 + C, lo:lo + GDN_DK] = (kh * jnp.exp(gc_last - gc_col)).astype(BF16)
        gl_ref[0, c, ln:ln + 1, :] = jnp.broadcast_to(jnp.exp(gc_last), (1, LANES))

    n = 2
    while n < C:
        for un in units:
            apow[un] = _dot(apow[un], apow[un]).astype(BF16)
        for un in units:
            pmat[un] = pmat[un] + _dot(pmat[un].astype(BF16), apow[un])
        n *= 2

    sols = {un: _dot(pmat[un].astype(BF16), rhs[un]) for un in units}
    for (c, h, d) in units:
        r0 = c * C
        ln = d * H + h
        lo = ln * GDN_DK
        sol = sols[c, h, d]
        u_ref[0, r0:r0 + C, lo:lo + GDN_DV] = sol[:, :GDN_DV]
        w_ref[0, r0:r0 + C, lo:lo + GDN_DK] = sol[:, GDN_DV:].astype(BF16)
        at_ref[0, r0:r0 + C, ln * C:(ln + 1) * C] = attn[c, h, d]


def _gdn_intra(q, k, v, g):
    B, S, _ = q.shape
    tc = min(128, S)
    tok = lambda b, i: (b, i, 0)
    wide = 2 * GDN_WIDTH
    return pl.pallas_call(
        _gdn_intra_kernel,
        grid=(B, S // tc),
        in_specs=[pl.BlockSpec((1, tc, GDN_WIDTH), tok)] * 3 + [pl.BlockSpec((1, tc, LANES), tok)],
        out_specs=[pl.BlockSpec((1, tc, wide), tok)] * 4
                  + [pl.BlockSpec((1, tc, 2 * GDN_HEADS * CHUNK), tok),
                     pl.BlockSpec((1, tc // CHUNK, 2 * GDN_HEADS, LANES), lambda b, i: (b, i, 0, 0))],
        out_shape=[jax.ShapeDtypeStruct((B, S, wide), F32)]
                  + [jax.ShapeDtypeStruct((B, S, wide), BF16)] * 3
                  + [jax.ShapeDtypeStruct((B, S, 2 * GDN_HEADS * CHUNK), BF16),
                     jax.ShapeDtypeStruct((B, S // CHUNK, 2 * GDN_HEADS, LANES), F32)],
        compiler_params=_params(("parallel", "parallel")),
        name="gdn_intra",
    )(q, k, v, g)


def _gdn_scan_kernel(uf_ref, ub_ref, wf_ref, wb_ref, qf_ref, qb_ref, kf_ref, kb_ref,
                     af_ref, ab_ref, gf_ref, gb_ref, of_ref, ob_ref, st_ref):
    n = pl.program_id(1)
    tb = uf_ref.shape[1]
    C = CHUNK
    H = GDN_HEADS
    nc = tb // C

    @pl.when(n == 0)
    def _():
        st_ref[...] = jnp.zeros_like(st_ref)

    refs = ((uf_ref, wf_ref, qf_ref, kf_ref, af_ref, gf_ref, of_ref),
            (ub_ref, wb_ref, qb_ref, kb_ref, ab_ref, gb_ref, ob_ref))
    chains = [(d, h) for d in range(2) for h in range(H)]
    for cc in range(nc):
        states, ms, vbs = {}, {}, {}
        for (d, h) in chains:
            u_ref, w_ref, qd_ref, kd_ref, at_ref, gl_ref, o_ref = refs[d]
            r0 = (cc if d == 0 else nc - 1 - cc) * C
            cols = slice(h * GDN_DK, (h + 1) * GDN_DK)
            states[d, h] = st_ref[d * H + h]
            wq = jnp.concatenate([w_ref[0, r0:r0 + C, cols], qd_ref[0, r0:r0 + C, cols]], axis=0)
            ms[d, h] = _dot(wq, states[d, h].astype(BF16))
        for (d, h) in chains:
            u_ref = refs[d][0]
            r0 = (cc if d == 0 else nc - 1 - cc) * C
            cols = slice(h * GDN_DK, (h + 1) * GDN_DK)
            vbs[d, h] = (u_ref[0, r0:r0 + C, cols] - ms[d, h][:C]).astype(BF16)
        for (d, h) in chains:
            u_ref, w_ref, qd_ref, kd_ref, at_ref, gl_ref, o_ref = refs[d]
            c = cc if d == 0 else nc - 1 - cc
            r0 = c * C
            ln = d * H + h
            cols = slice(h * GDN_DK, (h + 1) * GDN_DK)
            o_ref[0, r0:r0 + C, cols] = ms[d, h][C:] + _dot(at_ref[0, r0:r0 + C, h * C:(h + 1) * C], vbs[d, h])
            gl = gl_ref[0, c, ln:ln + 1, :]
            st_ref[ln] = states[d, h] * gl + _dot_tn(kd_ref[0, r0:r0 + C, cols], vbs[d, h])


def _gdn_scan(u, w, qd, kd, at, gl):
    B, S, _ = u.shape
    tb = min(256, S)
    nb = S // tb
    nc = tb // CHUNK
    fwd = lambda b, n: (b, n, 0)
    bwd = lambda b, n: (b, nb - 1 - n, 1)
    fwd4 = lambda b, n: (b, n, 0, 0)
    bwd4 = lambda b, n: (b, nb - 1 - n, 0, 0)
    wide = GDN_WIDTH
    aw = GDN_HEADS * CHUNK
    pair = lambda shape: [pl.BlockSpec(shape, fwd), pl.BlockSpec(shape, bwd)]
    glshape = (1, nc, 2 * GDN_HEADS, LANES)
    return pl.pallas_call(
        _gdn_scan_kernel,
        grid=(B, nb),
        in_specs=pair((1, tb, wide)) * 4 + pair((1, tb, aw))
                 + [pl.BlockSpec(glshape, fwd4), pl.BlockSpec(glshape, bwd4)],
        out_specs=[pl.BlockSpec((1, tb, wide), fwd),
                   pl.BlockSpec((1, tb, wide), lambda b, n: (b, nb - 1 - n, 0))],
        out_shape=[jax.ShapeDtypeStruct((B, S, wide), F32)] * 2,
        scratch_shapes=[pltpu.VMEM((2 * GDN_HEADS, GDN_DK, GDN_DV), F32)],
        compiler_params=_params(("parallel", "arbitrary")),
        name="gdn_scan",
    )(u, u, w, w, qd, qd, kd, kd, at, at, gl, gl)


def _attn_kernel(lambda_init, tk, qt_ref, k_ref, vt_ref, ct_ref, lp_ref, dn_ref, o_ref,
                 w_scr, pk_scr, sa_scr, sb_scr, pa_scr, pb_scr, acc_scr):
    i = pl.program_id(2)
    tq = qt_ref.shape[1]
    S = k_ref.shape[1]
    nk = S // tk
    half = tk // 2
    wide = 2 * tq

    qt = qt_ref[...]
    row = lax.broadcasted_iota(jnp.int32, qt.shape, 0)
    zero = jnp.zeros_like(qt)
    top = jnp.concatenate([jnp.where(row < DIFF_DH, qt, zero),
                           jnp.where(row >= DIFF_DH, qt, zero)], axis=1)
    ct = ct_ref[0]
    rep = wide // LANES
    rw = lax.broadcasted_iota(jnp.int32, (LANES, wide), 0)
    pos = jnp.where(rw == 0, jnp.tile(ct[0:1], (1, rep)),
                    jnp.where(rw == 1, jnp.tile(ct[1:2], (1, rep)),
                              jnp.where(rw == 2, jnp.tile(ct[2:3], (1, rep)), 0.0)))
    for kind, sign in enumerate((1.0, 0.0, -1.0)):
        w_scr[kind, 0:LANES, :] = top
        w_scr[kind, LANES:2 * LANES, :] = (sign * pos).astype(BF16)
    kr = lax.broadcasted_iota(jnp.int32, (tk, LANES), 0)
    kl = lax.broadcasted_iota(jnp.int32, (tk, LANES), 1)
    pk_scr[...] = jnp.where(kl < 3, (kr - half).astype(F32), 0.0).astype(BF16)
    acc_scr[...] = jnp.zeros_like(acc_scr)
    c = ct[3:4, 0:1]
    li = lax.broadcasted_iota(jnp.int32, (1, wide), 1)
    tpos = (jnp.where(li >= tq, li - tq, li) + i * tq).astype(F32)
    jd = (i * tq) // tk

    def scores(j, kind):
        j0 = pl.multiple_of(j * tk, tk)
        ka = jnp.concatenate([k_ref[0, pl.ds(j0, tk), :], pk_scr[...]], axis=1)
        return _dot(ka, w_scr[kind])

    def pv(j, p_ref):
        j0 = pl.multiple_of(j * tk, tk)
        return _dot(vt_ref[:, pl.ds(j0, tk)], p_ref[...])

    kpos = (jd * tk + lax.broadcasted_iota(jnp.int32, (tk, tq), 0)).astype(F32)
    qpos = (i * tq + lax.broadcasted_iota(jnp.int32, (tk, tq), 1)).astype(F32)
    bias = -c * jnp.abs(qpos - kpos)
    s = scores(jd, 1) + jnp.concatenate([bias, bias], axis=1)
    m0 = jnp.max(s, axis=0, keepdims=True)
    p = jnp.exp2(s - m0)
    l0 = jnp.sum(p, axis=0, keepdims=True)
    pb_scr[...] = p.astype(BF16)

    def tile(t):
        return jnp.where(t >= jd, t + 1, t)

    def kind_of(j):
        return jnp.where(j < jd, 0, 2)

    def half_step(t, s_cur, s_nxt, p_cur, p_prev, carry):
        m_run, l_run, alpha_prev, smax = carry
        j = tile(t)
        smax_nxt = smax
        if s_nxt is not None:
            jn = tile(t + 1)
            s_val = scores(jn, kind_of(jn))
            s_nxt[...] = s_val
            smax_nxt = jnp.max(s_val, axis=0, keepdims=True)
        sigma = jnp.where(j < jd, -1.0, 1.0).astype(F32)
        su = (sigma * c) * (tpos - (j * tk + half).astype(F32))
        m_new = jnp.maximum(m_run, smax + su)
        p = jnp.exp2(s_cur[...] - (m_new - su))
        alpha = jnp.exp2(m_run - m_new)
        l_new = alpha * l_run + jnp.sum(p, axis=0, keepdims=True)
        p_cur[...] = p.astype(BF16)
        jp = jnp.where(t == 0, jd, tile(t - 1))
        acc_scr[...] = acc_scr[...] * alpha_prev + pv(jp, p_prev)
        return m_new, l_new, alpha, smax_nxt

    def pair(pp, carry):
        carry = half_step(2 * pp, sa_scr, sb_scr, pa_scr, pb_scr, carry)
        return half_step(2 * pp + 1, sb_scr, sa_scr, pb_scr, pa_scr, carry)

    j_first = tile(0)
    s_first = scores(j_first, kind_of(j_first))
    sa_scr[...] = s_first
    carry = (m0, l0, jnp.ones((1, wide), F32), jnp.max(s_first, axis=0, keepdims=True))
    npairs = (nk - 2) // 2
    carry = lax.fori_loop(0, npairs, pair, carry, unroll=3 if npairs % 3 == 0 else 1)
    _, l_run, alpha_last, _ = half_step(nk - 2, sa_scr, None, pa_scr, pb_scr, carry)
    acc_scr[...] = acc_scr[...] * alpha_last + pv(tile(nk - 2), pa_scr)

    lp = lp_ref[...]
    lam = (jnp.exp(jnp.sum(lp[0:1] * lp[1:2], axis=1, keepdims=True))
           - jnp.exp(jnp.sum(lp[2:3] * lp[3:4], axis=1, keepdims=True)) + lambda_init)
    o_all = acc_scr[...] / l_run
    o = o_all[:, :tq] - lam * o_all[:, tq:]
    o = o * lax.rsqrt(jnp.mean(o * o, axis=0, keepdims=True) + EPS) * jnp.tile(dn_ref[...], (1, tq // LANES))
    o_ref[0] = (o * (1.0 - lambda_init)).T.astype(o_ref.dtype)


def _attention(dqt, dk, dvt, ctab, lp, dn, lambda_init):
    B, S, _ = dk.shape
    tk = min(512, S // 2)
    tq = min(256, tk)
    assert tk % tq == 0 and tk // 2 <= 256 and (S // tk) % 2 == 0
    nq = S // tq
    return pl.pallas_call(
        functools.partial(_attn_kernel, lambda_init, tk),
        grid=(B, DIFF_HEADS, nq),
        in_specs=[pl.BlockSpec((DIFF_DV, tq), lambda b, h, i: (h, b * nq + i)),
                  pl.BlockSpec((1, S, DIFF_DV), lambda b, h, i: (b, 0, h)),
                  pl.BlockSpec((DIFF_DV, S), lambda b, h, i: (h, b)),
                  pl.BlockSpec((1, SUBLANES, LANES), lambda b, h, i: (h, 0, 0)),
                  pl.BlockSpec((4, DIFF_DH), lambda b, h, i: (0, 0)),
                  pl.BlockSpec((DIFF_DV, LANES), lambda b, h, i: (0, 0))],
        out_specs=pl.BlockSpec((1, tq, DIFF_DV), lambda b, h, i: (b, i, h)),
        out_shape=jax.ShapeDtypeStruct((B, S, DIFF_WIDTH), BF16),
        scratch_shapes=[pltpu.VMEM((3, 2 * LANES, 2 * tq), BF16),
                        pltpu.VMEM((tk, LANES), BF16),
                        pltpu.VMEM((tk, 2 * tq), F32),
                        pltpu.VMEM((tk, 2 * tq), F32),
                        pltpu.VMEM((tk, 2 * tq), BF16),
                        pltpu.VMEM((tk, 2 * tq), BF16),
                        pltpu.VMEM((DIFF_DV, 2 * tq), F32)],
        compiler_params=_params(("parallel", "parallel", "parallel")),
        name="diff_attn",
    )(dqt, dk, dvt, ctab, lp, dn)


def _mix_kernel(x_ref, of_ref, ob_ref, gate_ref, oa_ref, gn_ref, wo_ref, y_ref):
    o = of_ref[...] + ob_ref[...]
    gate = gate_ref[...]
    parts = []
    for h in range(GDN_HEADS):
        cols = slice(h * GDN_DV, (h + 1) * GDN_DV)
        oh = o[:, cols]
        oh = oh * lax.rsqrt(jnp.mean(oh * oh, axis=-1, keepdims=True) + EPS) * gn_ref[...]
        gh = gate[:, cols]
        parts.append(oh * (gh * _sigmoid(gh)))
    og = jnp.concatenate(parts, axis=1).astype(BF16)
    y_ref[...] = (x_ref[...] + _dot(og, wo_ref[0:GDN_WIDTH, :])
                  + _dot(oa_ref[...], wo_ref[GDN_WIDTH:GDN_WIDTH + DIFF_WIDTH, :]))


def _mix(x2d, of, ob, gate, oa, gn, wo):
    T = x2d.shape[0]
    tm = min(512, T)
    row = lambda i: (i, 0)
    fixed = lambda i: (0, 0)
    half = pl.BlockSpec((tm, GDN_WIDTH), row)
    return pl.pallas_call(
        _mix_kernel,
        grid=(T // tm,),
        in_specs=[pl.BlockSpec((tm, D_MODEL), row), half, half, half, half,
                  pl.BlockSpec((1, GDN_DV), fixed),
                  pl.BlockSpec((GDN_WIDTH + DIFF_WIDTH, D_MODEL), fixed)],
        out_specs=pl.BlockSpec((tm, D_MODEL), row),
        out_shape=jax.ShapeDtypeStruct((T, D_MODEL), F32),
        compiler_params=_params(("parallel",)),
        name="mix_out",
    )(x2d, of, ob, gate, oa, gn, wo)


def _mlp_kernel(final, x_ref, nw_ref, wu_ref, wd_ref, nf_ref, y_ref):
    x = x_ref[...]
    ms = jnp.mean(x * x, axis=-1, keepdims=True)
    h = (x * lax.rsqrt(ms + EPS) * nw_ref[...]).astype(BF16)
    acc = x
    step = D_MODEL
    for c in range(D_FF // step):
        up = _dot(h, wu_ref[:, c * step:(c + 1) * step])
        act = jnp.square(jnp.maximum(up, 0.0)).astype(BF16)
        acc = acc + _dot(act, wd_ref[c * step:(c + 1) * step, :])
    if final:
        ms2 = jnp.mean(acc * acc, axis=-1, keepdims=True)
        acc = acc * lax.rsqrt(ms2 + EPS) * nf_ref[...]
    y_ref[...] = acc


def _mlp(x2d, nw, wu, wd, nf, final):
    T = x2d.shape[0]
    tm = min(512, T)
    row = lambda i: (i, 0)
    fixed = lambda i: (0, 0)
    return pl.pallas_call(
        functools.partial(_mlp_kernel, final),
        grid=(T // tm,),
        in_specs=[pl.BlockSpec((tm, D_MODEL), row),
                  pl.BlockSpec((1, D_MODEL), fixed),
                  pl.BlockSpec((D_MODEL, D_FF), fixed),
                  pl.BlockSpec((D_FF, D_MODEL), fixed),
                  pl.BlockSpec((1, D_MODEL), fixed)],
        out_specs=pl.BlockSpec((tm, D_MODEL), row),
        out_shape=jax.ShapeDtypeStruct((T, D_MODEL), F32),
        compiler_params=_params(("parallel",)),
        name="mlp",
    )(x2d, nw, wu, wd, nf)


def _prepare_weights(norm_mix, w_in, conv_w, a_log, dt_bias, gdn_norm, lambda_q1, lambda_k1,
                     lambda_q2, lambda_k2, diff_norm, w_out, norm_mlp, w_up, w_down, norm_final):
    layers = []
    ngate = 2 * GDN_HEADS
    q_lo, k_lo, v_lo = AB_HI, AB_HI + DIFF_WIDTH, AB_HI + 2 * DIFF_WIDTH
    for l in range(DEPTH):
        wm = jnp.concatenate([w_in[l][:, :AB_LO], w_in[l][:, k_lo:v_lo]], axis=1).astype(BF16)
        wab = jnp.pad(w_in[l][:, AB_LO:AB_HI], ((0, 0), (0, LANES - (AB_HI - AB_LO)))).astype(BF16)
        wqt = w_in[l][:, q_lo:k_lo].T.astype(BF16)
        wvt = w_in[l][:, v_lo:v_lo + DIFF_WIDTH].T.astype(BF16)
        gp = jnp.stack([jnp.pad(a_log[l].reshape(ngate).astype(F32), (0, LANES - ngate)),
                        jnp.pad(dt_bias[l].reshape(ngate).astype(F32), (0, LANES - ngate))])
        lp = jnp.stack([lambda_q1[l], lambda_k1[l], lambda_q2[l], lambda_k2[l]]).astype(F32)
        layers.append(dict(
            nmix=norm_mix[l].reshape(1, D_MODEL), wm=wm, wab=wab, wqt=wqt, wvt=wvt,
            cw=conv_w[l], gp=gp, gn=gdn_norm[l].reshape(1, GDN_DV),
            lp=lp, dn=jnp.broadcast_to(diff_norm[l].astype(F32)[:, None], (DIFF_DV, LANES)),
            wo=w_out[l].astype(BF16), nmlp=norm_mlp[l].reshape(1, D_MODEL),
            wu=w_up[l].astype(BF16), wd=w_down[l].astype(BF16)))
    return layers, jnp.asarray(_alibi_table()), norm_final.reshape(1, D_MODEL)


def _bf16_round(x):
    u = np.asarray(x, np.float32).view(np.uint32)
    u = (u + np.uint32(0x7FFF) + ((u >> np.uint32(16)) & np.uint32(1))) & np.uint32(0xFFFF0000)
    return u.view(np.float32)


def _alibi_table():
    tab = np.zeros((DIFF_HEADS, SUBLANES, LANES), np.float32)
    for h in range(DIFF_HEADS):
        c = np.float32(2.0 ** (-8.0 * (h + 1) / DIFF_HEADS) * LOG2E)
        hi = _bf16_round(c)
        mid = _bf16_round(np.float32(c - hi))
        lo = _bf16_round(np.float32(c - hi - mid))
        tab[h, 0], tab[h, 1], tab[h, 2] = hi, mid, lo
        tab[h, 3] = np.float32(np.float32(hi + mid) + lo)
    return tab


def _trunk(x, layers, ctab, nf):
    B, S, _ = x.shape
    T = B * S
    x2d = x.reshape(T, D_MODEL)
    for l, p in enumerate(layers):
        lambda_init = 0.8 - 0.6 * math.exp(-0.3 * l)
        qkv, gate, ab, dqt, dk, dvt = _inproj(x2d, p["nmix"], p["wm"], p["wab"], p["wqt"], p["wvt"])
        r3 = lambda t: t.reshape(B, S, t.shape[-1])
        q, k, v, g = _gdn_prep(r3(qkv), r3(ab), p["cw"], p["gp"])
        u, w, qd, kd, at, gl = _gdn_intra(q, k, v, g)
        of, ob = _gdn_scan(u, w, qd, kd, at, gl)
        oa = _attention(dqt, r3(dk), dvt, ctab, p["lp"], p["dn"], lambda_init)
        x2d = _mix(x2d, of.reshape(T, GDN_WIDTH), ob.reshape(T, GDN_WIDTH), gate,
                   oa.reshape(T, DIFF_WIDTH), p["gn"], p["wo"])
        x2d = _mlp(x2d, p["nmlp"], p["wu"], p["wd"], nf, l == DEPTH - 1)
    return x2d.reshape(B, S, D_MODEL)


def kernel(x_prompt, x_sample, norm_mix, w_in, conv_w, a_log, dt_bias, gdn_norm, lambda_q1,
           lambda_k1, lambda_q2, lambda_k2, diff_norm, w_out, norm_mlp, w_up, w_down, norm_final):
    layers, ctab, nf = _prepare_weights(
        norm_mix, w_in, conv_w, a_log, dt_bias, gdn_norm, lambda_q1, lambda_k1,
        lambda_q2, lambda_k2, diff_norm, w_out, norm_mlp, w_up, w_down, norm_final)
    return (_trunk(x_prompt, layers, ctab, nf), _trunk(x_sample, layers, ctab, nf))
```

```python
import functools
import math

import jax
import jax.numpy as jnp
import numpy as np
from jax import lax
from jax.experimental import pallas as pl
from jax.experimental.pallas import tpu as pltpu

D_MODEL = 1024
DEPTH = 2
GDN_HEADS = 4
GDN_DK = 128
GDN_DV = 128
GDN_WIDTH = GDN_HEADS * GDN_DV
CONV_W = 5
CONV_PAD = CONV_W // 2
CHUNK = 64
DIFF_HEADS = 4
DIFF_DH = 64
DIFF_DV = 2 * DIFF_DH
DIFF_WIDTH = DIFF_HEADS * DIFF_DV
D_FF = 4 * D_MODEL
EPS = 1e-6
QKV_W = 3 * GDN_WIDTH
AB_LO = QKV_W + GDN_WIDTH
AB_HI = AB_LO + 4 * GDN_HEADS
LANES = 128
SUBLANES = 8
VMEM_LIMIT = 56 * 1024 * 1024
NEG_BIG = -1e30
LOG2E = math.log2(math.e)
Q_SCALE = DIFF_DH ** -0.5 * LOG2E
SCAN_UNITS = 64
BF16 = jnp.bfloat16
F32 = jnp.float32


def _dot(a, b):
    return jnp.dot(a, b, preferred_element_type=F32)


def _dot_nt(a, b):
    return lax.dot_general(a, b, (((1,), (1,)), ((), ())), preferred_element_type=F32)


def _dot_tn(a, b):
    return lax.dot_general(a, b, (((0,), (0,)), ((), ())), preferred_element_type=F32)


def _sigmoid(x):
    return 1.0 / (1.0 + jnp.exp(-x))


def _params(sem):
    return pltpu.CompilerParams(dimension_semantics=sem, vmem_limit_bytes=VMEM_LIMIT)


def _inproj_kernel(x_ref, nw_ref, wm_ref, wab_ref, wqt_ref, wvt_ref,
                   qkv_ref, gate_ref, ab_ref, dqt_ref, dk_ref, dvt_ref):
    x = x_ref[...]
    ms = jnp.mean(x * x, axis=-1, keepdims=True)
    h = (x * lax.rsqrt(ms + EPS) * nw_ref[...]).astype(BF16)
    qkv_ref[...] = _dot(h, wm_ref[:, 0:QKV_W])
    gate_ref[...] = _dot(h, wm_ref[:, QKV_W:AB_LO])
    dk_ref[...] = _dot(h, wm_ref[:, AB_LO:AB_LO + DIFF_WIDTH]).astype(BF16)
    ab_ref[...] = _dot(h, wab_ref[...])
    dqt_ref[...] = (_dot_nt(wqt_ref[...], h) * Q_SCALE).astype(BF16)
    dvt_ref[...] = _dot_nt(wvt_ref[...], h).astype(BF16)


def _inproj(x2d, nw, wm, wab, wqt, wvt):
    T = x2d.shape[0]
    tm = min(512, T)
    row = lambda i: (i, 0)
    col = lambda i: (0, i)
    fixed = lambda i: (0, 0)
    nmain = wm.shape[1]
    return pl.pallas_call(
        _inproj_kernel,
        grid=(T // tm,),
        in_specs=[pl.BlockSpec((tm, D_MODEL), row),
                  pl.BlockSpec((1, D_MODEL), fixed),
                  pl.BlockSpec((D_MODEL, nmain), fixed),
                  pl.BlockSpec((D_MODEL, LANES), fixed),
                  pl.BlockSpec((DIFF_WIDTH, D_MODEL), fixed),
                  pl.BlockSpec((DIFF_WIDTH, D_MODEL), fixed)],
        out_specs=[pl.BlockSpec((tm, QKV_W), row),
                   pl.BlockSpec((tm, GDN_WIDTH), row),
                   pl.BlockSpec((tm, LANES), row),
                   pl.BlockSpec((DIFF_WIDTH, tm), col),
                   pl.BlockSpec((tm, DIFF_WIDTH), row),
                   pl.BlockSpec((DIFF_WIDTH, tm), col)],
        out_shape=[jax.ShapeDtypeStruct((T, QKV_W), F32),
                   jax.ShapeDtypeStruct((T, GDN_WIDTH), F32),
                   jax.ShapeDtypeStruct((T, LANES), F32),
                   jax.ShapeDtypeStruct((DIFF_WIDTH, T), BF16),
                   jax.ShapeDtypeStruct((T, DIFF_WIDTH), BF16),
                   jax.ShapeDtypeStruct((DIFF_WIDTH, T), BF16)],
        compiler_params=_params(("parallel",)),
        name="inproj",
    )(x2d, nw, wm, wab, wqt, wvt)


def _gdn_prep_kernel(cur_ref, prev_ref, next_ref, ab_ref, cw_ref, gp_ref,
                     q_ref, k_ref, v_ref, g_ref, xpad):
    i = pl.program_id(1)
    n = pl.num_programs(1)
    tm = cur_ref.shape[1]
    halo = SUBLANES
    xpad[halo:halo + tm, :] = cur_ref[0]
    xpad[0:halo, :] = jnp.where(i > 0, prev_ref[0], 0.0)
    xpad[halo + tm:2 * halo + tm, :] = jnp.where(i < n - 1, next_ref[0], 0.0)

    outs = (q_ref, k_ref, v_ref)
    for grp in range(3):
        c0 = grp * GDN_WIDTH
        acc = None
        for kk in range(CONV_W):
            r0 = halo - CONV_PAD + kk
            term = xpad[r0:r0 + tm, c0:c0 + GDN_WIDTH] * cw_ref[kk:kk + 1, c0:c0 + GDN_WIDTH]
            acc = term if acc is None else acc + term
        y = acc * _sigmoid(acc)
        if grp < 2:
            parts = []
            for h in range(GDN_HEADS):
                yh = y[:, h * GDN_DK:(h + 1) * GDN_DK]
                yh = yh * lax.rsqrt(jnp.sum(yh * yh, axis=-1, keepdims=True) + EPS)
                if grp == 0:
                    yh = yh * (GDN_DK ** -0.5)
                parts.append(yh)
            y = jnp.concatenate(parts, axis=1)
        outs[grp][0] = y

    ab = ab_ref[0]
    a_coef = -jnp.exp(gp_ref[0:1, :])
    z = ab + gp_ref[1:2, :]
    softplus = jnp.maximum(z, 0.0) + jnp.log1p(jnp.exp(-jnp.abs(z)))
    lane = lax.broadcasted_iota(jnp.int32, ab.shape, 1)
    ng = 2 * GDN_HEADS
    g_ref[0] = jnp.where(lane < ng, a_coef * softplus,
                         jnp.where(lane < 2 * ng, _sigmoid(ab), 0.0))


def _gdn_prep(qkv, ab, cw, gp):
    B, S, _ = qkv.shape
    tm = min(256, S)
    nb = tm // SUBLANES
    last = S // SUBLANES - 1
    tok = lambda b, i: (b, i, 0)
    fixed = lambda b, i: (0, 0)
    return pl.pallas_call(
        _gdn_prep_kernel,
        grid=(B, S // tm),
        in_specs=[pl.BlockSpec((1, tm, QKV_W), tok),
                  pl.BlockSpec((1, SUBLANES, QKV_W), lambda b, i: (b, jnp.maximum(i * nb - 1, 0), 0)),
                  pl.BlockSpec((1, SUBLANES, QKV_W), lambda b, i: (b, jnp.minimum((i + 1) * nb, last), 0)),
                  pl.BlockSpec((1, tm, LANES), tok),
                  pl.BlockSpec((CONV_W, QKV_W), fixed),
                  pl.BlockSpec((2, LANES), fixed)],
        out_specs=[pl.BlockSpec((1, tm, GDN_WIDTH), tok)] * 3 + [pl.BlockSpec((1, tm, LANES), tok)],
        out_shape=[jax.ShapeDtypeStruct((B, S, GDN_WIDTH), F32)] * 3
                  + [jax.ShapeDtypeStruct((B, S, LANES), F32)],
        scratch_shapes=[pltpu.VMEM((tm + 2 * SUBLANES, QKV_W), F32)],
        compiler_params=_params(("parallel", "parallel")),
        name="gdn_prep",
    )(qkv, qkv, qkv, ab, cw, gp)


def _gdn_intra_kernel(q_ref, k_ref, v_ref, g_ref,
                      u_ref, w_ref, qd_ref, kd_ref, at_ref, gl_ref):
    tc = q_ref.shape[1]
    C = CHUNK
    H = GDN_HEADS
    G = g_ref[0]
    pos = lax.broadcasted_iota(jnp.int32, G.shape, 0) & (C - 1)
    gcf = G
    gcb = G
    s = 1
    while s < C:
        gcf = gcf + jnp.where(pos >= s, pltpu.roll(gcf, s, axis=0), 0.0)
        gcb = gcb + jnp.where(pos < C - s, pltpu.roll(gcb, tc - s, axis=0), 0.0)
        s *= 2

    ri = lax.broadcasted_iota(jnp.int32, (C, C), 0)
    ci = lax.broadcasted_iota(jnp.int32, (C, C), 1)
    eye = (ri == ci).astype(F32)
    masks = ((ci <= ri, ci < ri, ri <= ci), (ci >= ri, ci > ri, ri >= ci))

    heads = [(c, h) for c in range(tc // C) for h in range(H)]
    units = [(c, h, d) for (c, h) in heads for d in range(2)]
    qs, ks, kkqk = {}, {}, {}
    for (c, h) in heads:
        r0 = c * C
        qs[c, h] = q_ref[0, r0:r0 + C, h * GDN_DK:(h + 1) * GDN_DK]
        ks[c, h] = k_ref[0, r0:r0 + C, h * GDN_DK:(h + 1) * GDN_DK]
    for (c, h) in heads:
        kq = jnp.concatenate([ks[c, h], qs[c, h]], axis=0).astype(BF16)
        kkqk[c, h] = _dot_nt(kq, ks[c, h].astype(BF16))

    pmat, apow, attn, rhs = {}, {}, {}, {}
    for (c, h, d) in units:
        r0 = c * C
        ln = d * H + h
        incl, strict, colsum_mask = masks[d]
        gc_all = gcf if d == 0 else gcb
        g_col = G[r0:r0 + C, ln:ln + 1]
        gc_col = gc_all[r0:r0 + C, ln:ln + 1]
        beta_col = G[r0:r0 + C, 2 * H + ln:2 * H + ln + 1]
        last = C - 1 if d == 0 else 0
        gc_last = gc_all[r0 + last:r0 + last + 1, ln:ln + 1]
        gc_row = jnp.sum(jnp.where(colsum_mask, g_col, 0.0), axis=0, keepdims=True)
        decay = jnp.exp(jnp.where(incl, gc_col - gc_row, NEG_BIG))
        e_col = jnp.exp(gc_col)
        m1 = kkqk[c, h]
        a = jnp.where(strict, (m1[:C] * beta_col) * decay, 0.0)
        attn[c, h, d] = (m1[C:] * decay).astype(BF16)
        pmat[c, h, d] = eye - a
        apow[c, h, d] = a.astype(BF16)
        kh = ks[c, h]
        vh = v_ref[0, r0:r0 + C, h * GDN_DV:(h + 1) * GDN_DV]
        rhs[c, h, d] = jnp.concatenate([vh * beta_col, kh * (beta_col * e_col)], axis=1).astype(BF16)
        lo = ln * GDN_DK
        qd_ref[0, r0:r0 + C, lo:lo + GDN_DK] = (qs[c, h] * e_col).astype(BF16)
        kd_ref[0, r0:r0 + C, lo:lo + GDN_DK] = (kh * jnp.exp(gc_last - gc_col)).astype(BF16)
        gl_ref[0, c, ln:ln + 1, :] = jnp.broadcast_to(jnp.exp(gc_last), (1, LANES))

    n = 2
    while n < C:
        for un in units:
            apow[un] = _dot(apow[un], apow[un]).astype(BF16)
        for un in units:
            pmat[un] = pmat[un] + _dot(pmat[un].astype(BF16), apow[un])
        n *= 2

    sols = {un: _dot(pmat[un].astype(BF16), rhs[un]) for un in units}
    for (c, h, d) in units:
        r0 = c * C
        ln = d * H + h
        lo = ln * GDN_DK
        sol = sols[c, h, d]
        u_ref[0, r0:r0 + C, lo:lo + GDN_DV] = sol[:, :GDN_DV]
        w_ref[0, r0:r0 + C, lo:lo + GDN_DK] = sol[:, GDN_DV:].astype(BF16)
        at_ref[0, r0:r0 + C, ln * C:(ln + 1) * C] = attn[c, h, d]


def _gdn_intra(q, k, v, g):
    B, S, _ = q.shape
    tc = min(256, S)
    tok = lambda b, i: (b, i, 0)
    wide = 2 * GDN_WIDTH
    return pl.pallas_call(
        _gdn_intra_kernel,
        grid=(B, S // tc),
        in_specs=[pl.BlockSpec((1, tc, GDN_WIDTH), tok)] * 3 + [pl.BlockSpec((1, tc, LANES), tok)],
        out_specs=[pl.BlockSpec((1, tc, wide), tok)] * 4
                  + [pl.BlockSpec((1, tc, 2 * GDN_HEADS * CHUNK), tok),
                     pl.BlockSpec((1, tc // CHUNK, 2 * GDN_HEADS, LANES), lambda b, i: (b, i, 0, 0))],
        out_shape=[jax.ShapeDtypeStruct((B, S, wide), F32)]
                  + [jax.ShapeDtypeStruct((B, S, wide), BF16)] * 3
                  + [jax.ShapeDtypeStruct((B, S, 2 * GDN_HEADS * CHUNK), BF16),
                     jax.ShapeDtypeStruct((B, S // CHUNK, 2 * GDN_HEADS, LANES), F32)],
        compiler_params=_params(("parallel", "parallel")),
        name="gdn_intra",
    )(q, k, v, g)


def _gdn_scan_kernel(uf_ref, ub_ref, wf_ref, wb_ref, qf_ref, qb_ref, kf_ref, kb_ref,
                     af_ref, ab_ref, gf_ref, gb_ref, of_ref, ob_ref, st_ref):
    n = pl.program_id(0)
    B, tb = uf_ref.shape[0], uf_ref.shape[1]
    C = CHUNK
    H = GDN_HEADS
    nc = tb // C

    @pl.when(n == 0)
    def _():
        st_ref[...] = jnp.zeros_like(st_ref)

    refs = ((uf_ref, wf_ref, qf_ref, kf_ref, af_ref, gf_ref, of_ref),
            (ub_ref, wb_ref, qb_ref, kb_ref, ab_ref, gb_ref, ob_ref))
    chains = [(b, d, h) for b in range(B) for d in range(2) for h in range(H)]
    for cc in range(nc):
        states, ms, vbs = {}, {}, {}
        for ch in chains:
            b, d, h = ch
            u_ref, w_ref, qd_ref, kd_ref, at_ref, gl_ref, o_ref = refs[d]
            r0 = (cc if d == 0 else nc - 1 - cc) * C
            cols = slice(h * GDN_DK, (h + 1) * GDN_DK)
            states[ch] = st_ref[(b * 2 + d) * H + h]
            wq = jnp.concatenate([w_ref[b, r0:r0 + C, cols], qd_ref[b, r0:r0 + C, cols]], axis=0)
            ms[ch] = _dot(wq, states[ch].astype(BF16))
        for ch in chains:
            b, d, h = ch
            u_ref = refs[d][0]
            r0 = (cc if d == 0 else nc - 1 - cc) * C
            cols = slice(h * GDN_DK, (h + 1) * GDN_DK)
            vbs[ch] = (u_ref[b, r0:r0 + C, cols] - ms[ch][:C]).astype(BF16)
        for ch in chains:
            b, d, h = ch
            u_ref, w_ref, qd_ref, kd_ref, at_ref, gl_ref, o_ref = refs[d]
            c = cc if d == 0 else nc - 1 - cc
            r0 = c * C
            ln = d * H + h
            cols = slice(h * GDN_DK, (h + 1) * GDN_DK)
            o_ref[b, r0:r0 + C, cols] = ms[ch][C:] + _dot(at_ref[b, r0:r0 + C, h * C:(h + 1) * C], vbs[ch])
            gl = gl_ref[b, c, ln:ln + 1, :]
            st_ref[(b * 2 + d) * H + h] = states[ch] * gl + _dot_tn(kd_ref[b, r0:r0 + C, cols], vbs[ch])


def _gdn_scan(u, w, qd, kd, at, gl):
    B, S, _ = u.shape
    tb = min(SCAN_UNITS // (B * 2 * GDN_HEADS) * CHUNK, S)
    nb = S // tb
    nc = tb // CHUNK
    fwd = lambda n: (0, n, 0)
    bwd = lambda n: (0, nb - 1 - n, 1)
    fwd4 = lambda n: (0, n, 0, 0)
    bwd4 = lambda n: (0, nb - 1 - n, 0, 0)
    wide = GDN_WIDTH
    aw = GDN_HEADS * CHUNK
    pair = lambda shape: [pl.BlockSpec(shape, fwd), pl.BlockSpec(shape, bwd)]
    glshape = (B, nc, 2 * GDN_HEADS, LANES)
    return pl.pallas_call(
        _gdn_scan_kernel,
        grid=(nb,),
        in_specs=pair((B, tb, wide)) * 4 + pair((B, tb, aw))
                 + [pl.BlockSpec(glshape, fwd4), pl.BlockSpec(glshape, bwd4)],
        out_specs=[pl.BlockSpec((B, tb, wide), fwd),
                   pl.BlockSpec((B, tb, wide), lambda n: (0, nb - 1 - n, 0))],
        out_shape=[jax.ShapeDtypeStruct((B, S, wide), F32)] * 2,
        scratch_shapes=[pltpu.VMEM((B * 2 * GDN_HEADS, GDN_DK, GDN_DV), F32)],
        compiler_params=_params(("arbitrary",)),
        name="gdn_scan",
    )(u, u, w, w, qd, qd, kd, kd, at, at, gl, gl)


def _attn_kernel(lambda_init, tk, qt_ref, k_ref, vt_ref, ct_ref, lp_ref, dn_ref, o_ref,
                 w_scr, pk_scr, sa_scr, sb_scr, pa_scr, pb_scr, acc_scr):
    i = pl.program_id(2)
    tq = qt_ref.shape[1]
    S = k_ref.shape[1]
    nk = S // tk
    half = tk // 2
    wide = 2 * tq

    qt = qt_ref[...]
    row = lax.broadcasted_iota(jnp.int32, qt.shape, 0)
    zero = jnp.zeros_like(qt)
    top = jnp.concatenate([jnp.where(row < DIFF_DH, qt, zero),
                           jnp.where(row >= DIFF_DH, qt, zero)], axis=1)
    ct = ct_ref[0]
    rep = wide // LANES
    rw = lax.broadcasted_iota(jnp.int32, (LANES, wide), 0)
    pos = jnp.where(rw == 0, jnp.tile(ct[0:1], (1, rep)),
                    jnp.where(rw == 1, jnp.tile(ct[1:2], (1, rep)),
                              jnp.where(rw == 2, jnp.tile(ct[2:3], (1, rep)), 0.0)))
    for kind, sign in enumerate((1.0, 0.0, -1.0)):
        w_scr[kind, 0:LANES, :] = top
        w_scr[kind, LANES:2 * LANES, :] = (sign * pos).astype(BF16)
    kr = lax.broadcasted_iota(jnp.int32, (tk, LANES), 0)
    kl = lax.broadcasted_iota(jnp.int32, (tk, LANES), 1)
    pk_scr[...] = jnp.where(kl < 3, (kr - half).astype(F32), 0.0).astype(BF16)
    acc_scr[...] = jnp.zeros_like(acc_scr)
    c = ct[3:4, 0:1]
    li = lax.broadcasted_iota(jnp.int32, (1, wide), 1)
    tpos = (jnp.where(li >= tq, li - tq, li) + i * tq).astype(F32)
    jd = (i * tq) // tk

    def scores(j, kind):
        j0 = pl.multiple_of(j * tk, tk)
        ka = jnp.concatenate([k_ref[0, pl.ds(j0, tk), :], pk_scr[...]], axis=1)
        return _dot(ka, w_scr[kind])

    def pv(j, p_ref):
        j0 = pl.multiple_of(j * tk, tk)
        return _dot(vt_ref[:, pl.ds(j0, tk)], p_ref[...])

    kpos = (jd * tk + lax.broadcasted_iota(jnp.int32, (tk, tq), 0)).astype(F32)
    qpos = (i * tq + lax.broadcasted_iota(jnp.int32, (tk, tq), 1)).astype(F32)
    bias = -c * jnp.abs(qpos - kpos)
    s = scores(jd, 1) + jnp.concatenate([bias, bias], axis=1)
    m0 = jnp.max(s, axis=0, keepdims=True)
    p = jnp.exp2(s - m0)
    l0 = jnp.sum(p, axis=0, keepdims=True)
    pb_scr[...] = p.astype(BF16)

    def tile(t):
        return jnp.where(t >= jd, t + 1, t)

    def kind_of(j):
        return jnp.where(j < jd, 0, 2)

    def half_step(t, s_cur, s_nxt, p_cur, p_prev, carry):
        m_run, l_run, alpha_prev, smax = carry
        j = tile(t)
        smax_nxt = smax
        if s_nxt is not None:
            jn = tile(t + 1)
            s_val = scores(jn, kind_of(jn))
            s_nxt[...] = s_val
            smax_nxt = jnp.max(s_val, axis=0, keepdims=True)
        sigma = jnp.where(j < jd, -1.0, 1.0).astype(F32)
        su = (sigma * c) * (tpos - (j * tk + half).astype(F32))
        m_new = jnp.maximum(m_run, smax + su)
        p = jnp.exp2(s_cur[...] - (m_new - su))
        alpha = jnp.exp2(m_run - m_new)
        l_new = alpha * l_run + jnp.sum(p, axis=0, keepdims=True)
        p_cur[...] = p.astype(BF16)
        jp = jnp.where(t == 0, jd, tile(t - 1))
        acc_scr[...] = acc_scr[...] * alpha_prev + pv(jp, p_prev)
        return m_new, l_new, alpha, smax_nxt

    def pair(pp, carry):
        carry = half_step(2 * pp, sa_scr, sb_scr, pa_scr, pb_scr, carry)
        return half_step(2 * pp + 1, sb_scr, sa_scr, pb_scr, pa_scr, carry)

    j_first = tile(0)
    s_first = scores(j_first, kind_of(j_first))
    sa_scr[...] = s_first
    carry = (m0, l0, jnp.ones((1, wide), F32), jnp.max(s_first, axis=0, keepdims=True))
    npairs = (nk - 2) // 2
    carry = lax.fori_loop(0, npairs, pair, carry, unroll=3 if npairs % 3 == 0 else 1)
    _, l_run, alpha_last, _ = half_step(nk - 2, sa_scr, None, pa_scr, pb_scr, carry)
    acc_scr[...] = acc_scr[...] * alpha_last + pv(tile(nk - 2), pa_scr)

    lp = lp_ref[...]
    lam = (jnp.exp(jnp.sum(lp[0:1] * lp[1:2], axis=1, keepdims=True))
           - jnp.exp(jnp.sum(lp[2:3] * lp[3:4], axis=1, keepdims=True)) + lambda_init)
    o_all = acc_scr[...] / l_run
    o = o_all[:, :tq] - lam * o_all[:, tq:]
    o = o * lax.rsqrt(jnp.mean(o * o, axis=0, keepdims=True) + EPS) * jnp.tile(dn_ref[...], (1, tq // LANES))
    o_ref[0] = (o * (1.0 - lambda_init)).T.astype(o_ref.dtype)


def _attention(dqt, dk, dvt, ctab, lp, dn, lambda_init):
    B, S, _ = dk.shape
    tk = min(512, S // 2)
    tq = min(256, tk)
    assert tk % tq == 0 and tk // 2 <= 256 and (S // tk) % 2 == 0
    nq = S // tq
    return pl.pallas_call(
        functools.partial(_attn_kernel, lambda_init, tk),
        grid=(B, DIFF_HEADS, nq),
        in_specs=[pl.BlockSpec((DIFF_DV, tq), lambda b, h, i: (h, b * nq + i)),
                  pl.BlockSpec((1, S, DIFF_DV), lambda b, h, i: (b, 0, h)),
                  pl.BlockSpec((DIFF_DV, S), lambda b, h, i: (h, b)),
                  pl.BlockSpec((1, SUBLANES, LANES), lambda b, h, i: (h, 0, 0)),
                  pl.BlockSpec((4, DIFF_DH), lambda b, h, i: (0, 0)),
                  pl.BlockSpec((DIFF_DV, LANES), lambda b, h, i: (0, 0))],
        out_specs=pl.BlockSpec((1, tq, DIFF_DV), lambda b, h, i: (b, i, h)),
        out_shape=jax.ShapeDtypeStruct((B, S, DIFF_WIDTH), BF16),
        scratch_shapes=[pltpu.VMEM((3, 2 * LANES, 2 * tq), BF16),
                        pltpu.VMEM((tk, LANES), BF16),
                        pltpu.VMEM((tk, 2 * tq), F32),
                        pltpu.VMEM((tk, 2 * tq), F32),
                        pltpu.VMEM((tk, 2 * tq), BF16),
                        pltpu.VMEM((tk, 2 * tq), BF16),
                        pltpu.VMEM((DIFF_DV, 2 * tq), F32)],
        compiler_params=_params(("parallel", "parallel", "parallel")),
        name="diff_attn",
    )(dqt, dk, dvt, ctab, lp, dn)


def _mix_kernel(x_ref, of_ref, ob_ref, gate_ref, oa_ref, gn_ref, wo_ref, y_ref):
    o = of_ref[...] + ob_ref[...]
    gate = gate_ref[...]
    parts = []
    for h in range(GDN_HEADS):
        cols = slice(h * GDN_DV, (h + 1) * GDN_DV)
        oh = o[:, cols]
        oh = oh * lax.rsqrt(jnp.mean(oh * oh, axis=-1, keepdims=True) + EPS) * gn_ref[...]
        gh = gate[:, cols]
        parts.append(oh * (gh * _sigmoid(gh)))
    og = jnp.concatenate(parts, axis=1).astype(BF16)
    y_ref[...] = (x_ref[...] + _dot(og, wo_ref[0:GDN_WIDTH, :])
                  + _dot(oa_ref[...], wo_ref[GDN_WIDTH:GDN_WIDTH + DIFF_WIDTH, :]))


def _mix(x2d, of, ob, gate, oa, gn, wo):
    T = x2d.shape[0]
    tm = min(512, T)
    row = lambda i: (i, 0)
    fixed = lambda i: (0, 0)
    half = pl.BlockSpec((tm, GDN_WIDTH), row)
    return pl.pallas_call(
        _mix_kernel,
        grid=(T // tm,),
        in_specs=[pl.BlockSpec((tm, D_MODEL), row), half, half, half, half,
                  pl.BlockSpec((1, GDN_DV), fixed),
                  pl.BlockSpec((GDN_WIDTH + DIFF_WIDTH, D_MODEL), fixed)],
        out_specs=pl.BlockSpec((tm, D_MODEL), row),
        out_shape=jax.ShapeDtypeStruct((T, D_MODEL), F32),
        compiler_params=_params(("parallel",)),
        name="mix_out",
    )(x2d, of, ob, gate, oa, gn, wo)


def _mlp_kernel(final, x_ref, nw_ref, wu_ref, wd_ref, nf_ref, y_ref):
    x = x_ref[...]
    ms = jnp.mean(x * x, axis=-1, keepdims=True)
    h = (x * lax.rsqrt(ms + EPS) * nw_ref[...]).astype(BF16)
    acc = x
    step = D_MODEL
    for c in range(D_FF // step):
        up = _dot(h, wu_ref[:, c * step:(c + 1) * step])
        act = jnp.square(jnp.maximum(up, 0.0)).astype(BF16)
        acc = acc + _dot(act, wd_ref[c * step:(c + 1) * step, :])
    if final:
        ms2 = jnp.mean(acc * acc, axis=-1, keepdims=True)
        acc = acc * lax.rsqrt(ms2 + EPS) * nf_ref[...]
    y_ref[...] = acc


def _mlp(x2d, nw, wu, wd, nf, final):
    T = x2d.shape[0]
    tm = min(512, T)
    row = lambda i: (i, 0)
    fixed = lambda i: (0, 0)
    return pl.pallas_call(
        functools.partial(_mlp_kernel, final),
        grid=(T // tm,),
        in_specs=[pl.BlockSpec((tm, D_MODEL), row),
                  pl.BlockSpec((1, D_MODEL), fixed),
                  pl.BlockSpec((D_MODEL, D_FF), fixed),
                  pl.BlockSpec((D_FF, D_MODEL), fixed),
                  pl.BlockSpec((1, D_MODEL), fixed)],
        out_specs=pl.BlockSpec((tm, D_MODEL), row),
        out_shape=jax.ShapeDtypeStruct((T, D_MODEL), F32),
        compiler_params=_params(("parallel",)),
        name="mlp",
    )(x2d, nw, wu, wd, nf)


def _prepare_weights(norm_mix, w_in, conv_w, a_log, dt_bias, gdn_norm, lambda_q1, lambda_k1,
                     lambda_q2, lambda_k2, diff_norm, w_out, norm_mlp, w_up, w_down, norm_final):
    layers = []
    ngate = 2 * GDN_HEADS
    q_lo, k_lo, v_lo = AB_HI, AB_HI + DIFF_WIDTH, AB_HI + 2 * DIFF_WIDTH
    for l in range(DEPTH):
        wm = jnp.concatenate([w_in[l][:, :AB_LO], w_in[l][:, k_lo:v_lo]], axis=1).astype(BF16)
        wab = jnp.pad(w_in[l][:, AB_LO:AB_HI], ((0, 0), (0, LANES - (AB_HI - AB_LO)))).astype(BF16)
        wqt = w_in[l][:, q_lo:k_lo].T.astype(BF16)
        wvt = w_in[l][:, v_lo:v_lo + DIFF_WIDTH].T.astype(BF16)
        gp = jnp.stack([jnp.pad(a_log[l].reshape(ngate).astype(F32), (0, LANES - ngate)),
                        jnp.pad(dt_bias[l].reshape(ngate).astype(F32), (0, LANES - ngate))])
        lp = jnp.stack([lambda_q1[l], lambda_k1[l], lambda_q2[l], lambda_k2[l]]).astype(F32)
        layers.append(dict(
            nmix=norm_mix[l].reshape(1, D_MODEL), wm=wm, wab=wab, wqt=wqt, wvt=wvt,
            cw=conv_w[l], gp=gp, gn=gdn_norm[l].reshape(1, GDN_DV),
            lp=lp, dn=jnp.broadcast_to(diff_norm[l].astype(F32)[:, None], (DIFF_DV, LANES)),
            wo=w_out[l].astype(BF16), nmlp=norm_mlp[l].reshape(1, D_MODEL),
            wu=w_up[l].astype(BF16), wd=w_down[l].astype(BF16)))
    return layers, jnp.asarray(_alibi_table()), norm_final.reshape(1, D_MODEL)


def _bf16_round(x):
    u = np.asarray(x, np.float32).view(np.uint32)
    u = (u + np.uint32(0x7FFF) + ((u >> np.uint32(16)) & np.uint32(1))) & np.uint32(0xFFFF0000)
    return u.view(np.float32)


def _alibi_table():
    tab = np.zeros((DIFF_HEADS, SUBLANES, LANES), np.float32)
    for h in range(DIFF_HEADS):
        c = np.float32(2.0 ** (-8.0 * (h + 1) / DIFF_HEADS) * LOG2E)
        hi = _bf16_round(c)
        mid = _bf16_round(np.float32(c - hi))
        lo = _bf16_round(np.float32(c - hi - mid))
        tab[h, 0], tab[h, 1], tab[h, 2] = hi, mid, lo
        tab[h, 3] = np.float32(np.float32(hi + mid) + lo)
    return tab


def _trunk(x, layers, ctab, nf):
    B, S, _ = x.shape
    T = B * S
    x2d = x.reshape(T, D_MODEL)
    for l, p in enumerate(layers):
        lambda_init = 0.8 - 0.6 * math.exp(-0.3 * l)
        qkv, gate, ab, dqt, dk, dvt = _inproj(x2d, p["nmix"], p["wm"], p["wab"], p["wqt"], p["wvt"])
        r3 = lambda t: t.reshape(B, S, t.shape[-1])
        q, k, v, g = _gdn_prep(r3(qkv), r3(ab), p["cw"], p["gp"])
        u, w, qd, kd, at, gl = _gdn_intra(q, k, v, g)
        of, ob = _gdn_scan(u, w, qd, kd, at, gl)
        oa = _attention(dqt, r3(dk), dvt, ctab, p["lp"], p["dn"], lambda_init)
        x2d = _mix(x2d, of.reshape(T, GDN_WIDTH), ob.reshape(T, GDN_WIDTH), gate,
                   oa.reshape(T, DIFF_WIDTH), p["gn"], p["wo"])
        x2d = _mlp(x2d, p["nmlp"], p["wu"], p["wd"], nf, l == DEPTH - 1)
    return x2d.reshape(B, S, D_MODEL)


def kernel(x_prompt, x_sample, norm_mix, w_in, conv_w, a_log, dt_bias, gdn_norm, lambda_q1,
           lambda_k1, lambda_q2, lambda_k2, diff_norm, w_out, norm_mlp, w_up, w_down, norm_final):
    layers, ctab, nf = _prepare_weights(
        norm_mix, w_in, conv_w, a_log, dt_bias, gdn_norm, lambda_q1, lambda_k1,
        lambda_q2, lambda_k2, diff_norm, w_out, norm_mlp, w_up, w_down, norm_final)
    return (_trunk(x_prompt, layers, ctab, nf), _trunk(x_sample, layers, ctab, nf))
```

```python
import functools
import math

import jax
import jax.numpy as jnp
import numpy as np
from jax import lax
from jax.experimental import pallas as pl
from jax.experimental.pallas import tpu as pltpu

D_MODEL = 1024
DEPTH = 2
GDN_HEADS = 4
GDN_DK = 128
GDN_DV = 128
GDN_WIDTH = GDN_HEADS * GDN_DV
CONV_W = 5
CONV_PAD = CONV_W // 2
CHUNK = 64
DIFF_HEADS = 4
DIFF_DH = 64
DIFF_DV = 2 * DIFF_DH
DIFF_WIDTH = DIFF_HEADS * DIFF_DV
D_FF = 4 * D_MODEL
EPS = 1e-6
QKV_W = 3 * GDN_WIDTH
AB_LO = QKV_W + GDN_WIDTH
AB_HI = AB_LO + 4 * GDN_HEADS
LANES = 128
SUBLANES = 8
VMEM_LIMIT = 56 * 1024 * 1024
NEG_BIG = -1e30
LOG2E = math.log2(math.e)
Q_SCALE = DIFF_DH ** -0.5 * LOG2E
SCAN_UNITS = 64
BF16 = jnp.bfloat16
F32 = jnp.float32


def _dot(a, b):
    return jnp.dot(a, b, preferred_element_type=F32)


def _dot_nt(a, b):
    return lax.dot_general(a, b, (((1,), (1,)), ((), ())), preferred_element_type=F32)


def _dot_tn(a, b):
    return lax.dot_general(a, b, (((0,), (0,)), ((), ())), preferred_element_type=F32)


def _sigmoid(x):
    return 1.0 / (1.0 + jnp.exp(-x))


def _params(sem):
    return pltpu.CompilerParams(dimension_semantics=sem, vmem_limit_bytes=VMEM_LIMIT)


def _inproj_kernel(x_ref, nw_ref, wm_ref, wab_ref, wqt_ref, wvt_ref,
                   qkv_ref, gate_ref, ab_ref, dqt_ref, dk_ref, dvt_ref):
    x = x_ref[...]
    ms = jnp.mean(x * x, axis=-1, keepdims=True)
    h = (x * lax.rsqrt(ms + EPS) * nw_ref[...]).astype(BF16)
    qkv_ref[...] = _dot(h, wm_ref[:, 0:QKV_W])
    gate_ref[...] = _dot(h, wm_ref[:, QKV_W:AB_LO])
    dk_ref[...] = _dot(h, wm_ref[:, AB_LO:AB_LO + DIFF_WIDTH]).astype(BF16)
    ab_ref[...] = _dot(h, wab_ref[...])
    dqt_ref[...] = (_dot_nt(wqt_ref[...], h) * Q_SCALE).astype(BF16)
    dvt_ref[...] = _dot_nt(wvt_ref[...], h).astype(BF16)


def _inproj(x2d, nw, wm, wab, wqt, wvt):
    T = x2d.shape[0]
    tm = min(512, T)
    row = lambda i: (i, 0)
    col = lambda i: (0, i)
    fixed = lambda i: (0, 0)
    nmain = wm.shape[1]
    return pl.pallas_call(
        _inproj_kernel,
        grid=(T // tm,),
        in_specs=[pl.BlockSpec((tm, D_MODEL), row),
                  pl.BlockSpec((1, D_MODEL), fixed),
                  pl.BlockSpec((D_MODEL, nmain), fixed),
                  pl.BlockSpec((D_MODEL, LANES), fixed),
                  pl.BlockSpec((DIFF_WIDTH, D_MODEL), fixed),
                  pl.BlockSpec((DIFF_WIDTH, D_MODEL), fixed)],
        out_specs=[pl.BlockSpec((tm, QKV_W), row),
                   pl.BlockSpec((tm, GDN_WIDTH), row),
                   pl.BlockSpec((tm, LANES), row),
                   pl.BlockSpec((DIFF_WIDTH, tm), col),
                   pl.BlockSpec((tm, DIFF_WIDTH), row),
                   pl.BlockSpec((DIFF_WIDTH, tm), col)],
        out_shape=[jax.ShapeDtypeStruct((T, QKV_W), F32),
                   jax.ShapeDtypeStruct((T, GDN_WIDTH), F32),
                   jax.ShapeDtypeStruct((T, LANES), F32),
                   jax.ShapeDtypeStruct((DIFF_WIDTH, T), BF16),
                   jax.ShapeDtypeStruct((T, DIFF_WIDTH), BF16),
                   jax.ShapeDtypeStruct((DIFF_WIDTH, T), BF16)],
        compiler_params=_params(("parallel",)),
        name="inproj",
    )(x2d, nw, wm, wab, wqt, wvt)


def _gdn_prep_kernel(cur_ref, prev_ref, next_ref, ab_ref, cw_ref, gp_ref,
                     q_ref, k_ref, v_ref, g_ref, xpad):
    i = pl.program_id(1)
    n = pl.num_programs(1)
    tm = cur_ref.shape[1]
    halo = SUBLANES
    xpad[halo:halo + tm, :] = cur_ref[0]
    xpad[0:halo, :] = jnp.where(i > 0, prev_ref[0], 0.0)
    xpad[halo + tm:2 * halo + tm, :] = jnp.where(i < n - 1, next_ref[0], 0.0)

    outs = (q_ref, k_ref, v_ref)
    for grp in range(3):
        c0 = grp * GDN_WIDTH
        acc = None
        for kk in range(CONV_W):
            r0 = halo - CONV_PAD + kk
            term = xpad[r0:r0 + tm, c0:c0 + GDN_WIDTH] * cw_ref[kk:kk + 1, c0:c0 + GDN_WIDTH]
            acc = term if acc is None else acc + term
        y = acc * _sigmoid(acc)
        if grp < 2:
            parts = []
            for h in range(GDN_HEADS):
                yh = y[:, h * GDN_DK:(h + 1) * GDN_DK]
                yh = yh * lax.rsqrt(jnp.sum(yh * yh, axis=-1, keepdims=True) + EPS)
                if grp == 0:
                    yh = yh * (GDN_DK ** -0.5)
                parts.append(yh)
            y = jnp.concatenate(parts, axis=1)
        outs[grp][0] = y

    ab = ab_ref[0]
    a_coef = -jnp.exp(gp_ref[0:1, :])
    z = ab + gp_ref[1:2, :]
    softplus = jnp.maximum(z, 0.0) + jnp.log1p(jnp.exp(-jnp.abs(z)))
    lane = lax.broadcasted_iota(jnp.int32, ab.shape, 1)
    ng = 2 * GDN_HEADS
    g_ref[0] = jnp.where(lane < ng, a_coef * softplus,
                         jnp.where(lane < 2 * ng, _sigmoid(ab), 0.0))


def _gdn_prep(qkv, ab, cw, gp):
    B, S, _ = qkv.shape
    tm = min(256, S)
    nb = tm // SUBLANES
    last = S // SUBLANES - 1
    tok = lambda b, i: (b, i, 0)
    fixed = lambda b, i: (0, 0)
    return pl.pallas_call(
        _gdn_prep_kernel,
        grid=(B, S // tm),
        in_specs=[pl.BlockSpec((1, tm, QKV_W), tok),
                  pl.BlockSpec((1, SUBLANES, QKV_W), lambda b, i: (b, jnp.maximum(i * nb - 1, 0), 0)),
                  pl.BlockSpec((1, SUBLANES, QKV_W), lambda b, i: (b, jnp.minimum((i + 1) * nb, last), 0)),
                  pl.BlockSpec((1, tm, LANES), tok),
                  pl.BlockSpec((CONV_W, QKV_W), fixed),
                  pl.BlockSpec((2, LANES), fixed)],
        out_specs=[pl.BlockSpec((1, tm, GDN_WIDTH), tok)] * 3 + [pl.BlockSpec((1, tm, LANES), tok)],
        out_shape=[jax.ShapeDtypeStruct((B, S, GDN_WIDTH), F32)] * 3
                  + [jax.ShapeDtypeStruct((B, S, LANES), F32)],
        scratch_shapes=[pltpu.VMEM((tm + 2 * SUBLANES, QKV_W), F32)],
        compiler_params=_params(("parallel", "parallel")),
        name="gdn_prep",
    )(qkv, qkv, qkv, ab, cw, gp)


def _gdn_intra_kernel(q_ref, k_ref, v_ref, g_ref,
                      u_ref, w_ref, qd_ref, kd_ref, at_ref, gl_ref):
    tc = q_ref.shape[1]
    C = CHUNK
    H = GDN_HEADS
    G = g_ref[0]
    pos = lax.broadcasted_iota(jnp.int32, G.shape, 0) & (C - 1)
    gcf = G
    gcb = G
    s = 1
    while s < C:
        gcf = gcf + jnp.where(pos >= s, pltpu.roll(gcf, s, axis=0), 0.0)
        gcb = gcb + jnp.where(pos < C - s, pltpu.roll(gcb, tc - s, axis=0), 0.0)
        s *= 2

    ri = lax.broadcasted_iota(jnp.int32, (C, C), 0)
    ci = lax.broadcasted_iota(jnp.int32, (C, C), 1)
    eye = (ri == ci).astype(F32)
    masks = ((ci <= ri, ci < ri, ri <= ci), (ci >= ri, ci > ri, ri >= ci))

    heads = [(c, h) for c in range(tc // C) for h in range(H)]
    units = [(c, h, d) for (c, h) in heads for d in range(2)]
    qs, ks, kkqk = {}, {}, {}
    for (c, h) in heads:
        r0 = c * C
        qs[c, h] = q_ref[0, r0:r0 + C, h * GDN_DK:(h + 1) * GDN_DK]
        ks[c, h] = k_ref[0, r0:r0 + C, h * GDN_DK:(h + 1) * GDN_DK]
    for (c, h) in heads:
        kq = jnp.concatenate([ks[c, h], qs[c, h]], axis=0).astype(BF16)
        kkqk[c, h] = _dot_nt(kq, ks[c, h].astype(BF16))

    pmat, amat, attn, rhs = {}, {}, {}, {}
    for (c, h, d) in units:
        r0 = c * C
        ln = d * H + h
        incl, strict, colsum_mask = masks[d]
        gc_all = gcf if d == 0 else gcb
        g_col = G[r0:r0 + C, ln:ln + 1]
        gc_col = gc_all[r0:r0 + C, ln:ln + 1]
        beta_col = G[r0:r0 + C, 2 * H + ln:2 * H + ln + 1]
        last = C - 1 if d == 0 else 0
        gc_last = gc_all[r0 + last:r0 + last + 1, ln:ln + 1]
        gc_row = jnp.sum(jnp.where(colsum_mask, g_col, 0.0), axis=0, keepdims=True)
        decay = jnp.exp(jnp.where(incl, gc_col - gc_row, NEG_BIG))
        e_col = jnp.exp(gc_col)
        m1 = kkqk[c, h]
        a = jnp.where(strict, (m1[:C] * beta_col) * decay, 0.0)
        attn[c, h, d] = (m1[C:] * decay).astype(BF16)
        amat[c, h, d] = a
        kh = ks[c, h]
        vh = v_ref[0, r0:r0 + C, h * GDN_DV:(h + 1) * GDN_DV]
        rhs[c, h, d] = jnp.concatenate([vh * beta_col, kh * (beta_col * e_col)], axis=1).astype(BF16)
        lo = ln * GDN_DK
        qd_ref[0, r0:r0 + C, lo:lo + GDN_DK] = (qs[c, h] * e_col).astype(BF16)
        kd_ref[0, r0:r0 + C, lo:lo + GDN_DK] = (kh * jnp.exp(gc_last - gc_col)).astype(BF16)
        gl_ref[0, c, ln:ln + 1, :] = jnp.broadcast_to(jnp.exp(gc_last), (1, LANES))

    for un in units:
        pmat[un] = eye - jnp.where((ri >> 1) == (ci >> 1), amat[un], 0.0)
    lb = 1
    while (1 << lb) < C:
        coupling = ((ri >> (lb + 1)) == (ci >> (lb + 1))) & ((ri >> lb) != (ci >> lb))
        xs = {}
        for un in units:
            l_b = jnp.where(coupling, amat[un], 0.0).astype(BF16)
            xs[un] = _dot(l_b, pmat[un].astype(BF16)).astype(BF16)
        for un in units:
            pmat[un] = pmat[un] - _dot(pmat[un].astype(BF16), xs[un])
        lb += 1

    sols = {un: _dot(pmat[un].astype(BF16), rhs[un]) for un in units}
    for (c, h, d) in units:
        r0 = c * C
        ln = d * H + h
        lo = ln * GDN_DK
        sol = sols[c, h, d]
        u_ref[0, r0:r0 + C, lo:lo + GDN_DV] = sol[:, :GDN_DV]
        w_ref[0, r0:r0 + C, lo:lo + GDN_DK] = sol[:, GDN_DV:].astype(BF16)
        at_ref[0, r0:r0 + C, ln * C:(ln + 1) * C] = attn[c, h, d]


def _gdn_intra(q, k, v, g):
    B, S, _ = q.shape
    tc = min(256, S)
    tok = lambda b, i: (b, i, 0)
    wide = 2 * GDN_WIDTH
    return pl.pallas_call(
        _gdn_intra_kernel,
        grid=(B, S // tc),
        in_specs=[pl.BlockSpec((1, tc, GDN_WIDTH), tok)] * 3 + [pl.BlockSpec((1, tc, LANES), tok)],
        out_specs=[pl.BlockSpec((1, tc, wide), tok)] * 4
                  + [pl.BlockSpec((1, tc, 2 * GDN_HEADS * CHUNK), tok),
                     pl.BlockSpec((1, tc // CHUNK, 2 * GDN_HEADS, LANES), lambda b, i: (b, i, 0, 0))],
        out_shape=[jax.ShapeDtypeStruct((B, S, wide), F32)]
                  + [jax.ShapeDtypeStruct((B, S, wide), BF16)] * 3
                  + [jax.ShapeDtypeStruct((B, S, 2 * GDN_HEADS * CHUNK), BF16),
                     jax.ShapeDtypeStruct((B, S // CHUNK, 2 * GDN_HEADS, LANES), F32)],
        compiler_params=_params(("parallel", "parallel")),
        name="gdn_intra",
    )(q, k, v, g)


def _gdn_scan_kernel(uf_ref, ub_ref, wf_ref, wb_ref, qf_ref, qb_ref, kf_ref, kb_ref,
                     af_ref, ab_ref, gf_ref, gb_ref, of_ref, ob_ref, st_ref):
    n = pl.program_id(0)
    B, tb = uf_ref.shape[0], uf_ref.shape[1]
    C = CHUNK
    H = GDN_HEADS
    nc = tb // C

    @pl.when(n == 0)
    def _():
        st_ref[...] = jnp.zeros_like(st_ref)

    refs = ((uf_ref, wf_ref, qf_ref, kf_ref, af_ref, gf_ref, of_ref),
            (ub_ref, wb_ref, qb_ref, kb_ref, ab_ref, gb_ref, ob_ref))
    chains = [(b, d, h) for b in range(B) for d in range(2) for h in range(H)]
    for cc in range(nc):
        states, ms, vbs = {}, {}, {}
        for ch in chains:
            b, d, h = ch
            u_ref, w_ref, qd_ref, kd_ref, at_ref, gl_ref, o_ref = refs[d]
            r0 = (cc if d == 0 else nc - 1 - cc) * C
            cols = slice(h * GDN_DK, (h + 1) * GDN_DK)
            states[ch] = st_ref[(b * 2 + d) * H + h]
            wq = jnp.concatenate([w_ref[b, r0:r0 + C, cols], qd_ref[b, r0:r0 + C, cols]], axis=0)
            ms[ch] = _dot(wq, states[ch].astype(BF16))
        for ch in chains:
            b, d, h = ch
            u_ref = refs[d][0]
            r0 = (cc if d == 0 else nc - 1 - cc) * C
            cols = slice(h * GDN_DK, (h + 1) * GDN_DK)
            vbs[ch] = (u_ref[b, r0:r0 + C, cols] - ms[ch][:C]).astype(BF16)
        for ch in chains:
            b, d, h = ch
            u_ref, w_ref, qd_ref, kd_ref, at_ref, gl_ref, o_ref = refs[d]
            c = cc if d == 0 else nc - 1 - cc
            r0 = c * C
            ln = d * H + h
            cols = slice(h * GDN_DK, (h + 1) * GDN_DK)
            o_ref[b, r0:r0 + C, cols] = ms[ch][C:] + _dot(at_ref[b, r0:r0 + C, h * C:(h + 1) * C], vbs[ch])
            gl = gl_ref[b, c, ln:ln + 1, :]
            st_ref[(b * 2 + d) * H + h] = states[ch] * gl + _dot_tn(kd_ref[b, r0:r0 + C, cols], vbs[ch])


def _gdn_scan(u, w, qd, kd, at, gl):
    B, S, _ = u.shape
    tb = min(SCAN_UNITS // (B * 2 * GDN_HEADS) * CHUNK, S)
    nb = S // tb
    nc = tb // CHUNK
    fwd = lambda n: (0, n, 0)
    bwd = lambda n: (0, nb - 1 - n, 1)
    fwd4 = lambda n: (0, n, 0, 0)
    bwd4 = lambda n: (0, nb - 1 - n, 0, 0)
    wide = GDN_WIDTH
    aw = GDN_HEADS * CHUNK
    pair = lambda shape: [pl.BlockSpec(shape, fwd), pl.BlockSpec(shape, bwd)]
    glshape = (B, nc, 2 * GDN_HEADS, LANES)
    return pl.pallas_call(
        _gdn_scan_kernel,
        grid=(nb,),
        in_specs=pair((B, tb, wide)) * 4 + pair((B, tb, aw))
                 + [pl.BlockSpec(glshape, fwd4), pl.BlockSpec(glshape, bwd4)],
        out_specs=[pl.BlockSpec((B, tb, wide), fwd),
                   pl.BlockSpec((B, tb, wide), lambda n: (0, nb - 1 - n, 0))],
        out_shape=[jax.ShapeDtypeStruct((B, S, wide), F32)] * 2,
        scratch_shapes=[pltpu.VMEM((B * 2 * GDN_HEADS, GDN_DK, GDN_DV), F32)],
        compiler_params=_params(("arbitrary",)),
        name="gdn_scan",
    )(u, u, w, w, qd, qd, kd, kd, at, at, gl, gl)


def _attn_kernel(lambda_init, tk, qt_ref, k_ref, vt_ref, ct_ref, lp_ref, dn_ref, o_ref,
                 w_scr, pk_scr, sa_scr, sb_scr, pa_scr, pb_scr, acc_scr):
    i = pl.program_id(2)
    tq = qt_ref.shape[1]
    S = k_ref.shape[1]
    nk = S // tk
    half = tk // 2
    wide = 2 * tq

    qt = qt_ref[...]
    row = lax.broadcasted_iota(jnp.int32, qt.shape, 0)
    zero = jnp.zeros_like(qt)
    top = jnp.concatenate([jnp.where(row < DIFF_DH, qt, zero),
                           jnp.where(row >= DIFF_DH, qt, zero)], axis=1)
    ct = ct_ref[0]
    rep = wide // LANES
    rw = lax.broadcasted_iota(jnp.int32, (LANES, wide), 0)
    pos = jnp.where(rw == 0, jnp.tile(ct[0:1], (1, rep)),
                    jnp.where(rw == 1, jnp.tile(ct[1:2], (1, rep)),
                              jnp.where(rw == 2, jnp.tile(ct[2:3], (1, rep)), 0.0)))
    for kind, sign in enumerate((1.0, 0.0, -1.0)):
        w_scr[kind, 0:LANES, :] = top
        w_scr[kind, LANES:2 * LANES, :] = (sign * pos).astype(BF16)
    kr = lax.broadcasted_iota(jnp.int32, (tk, LANES), 0)
    kl = lax.broadcasted_iota(jnp.int32, (tk, LANES), 1)
    pk_scr[...] = jnp.where(kl < 3, (kr - half).astype(F32), 0.0).astype(BF16)
    acc_scr[...] = jnp.zeros_like(acc_scr)
    c = ct[3:4, 0:1]
    li = lax.broadcasted_iota(jnp.int32, (1, wide), 1)
    tpos = (jnp.where(li >= tq, li - tq, li) + i * tq).astype(F32)
    jd = (i * tq) // tk

    def scores(j, kind):
        j0 = pl.multiple_of(j * tk, tk)
        ka = jnp.concatenate([k_ref[0, pl.ds(j0, tk), :], pk_scr[...]], axis=1)
        return _dot(ka, w_scr[kind])

    def pv(j, p_ref):
        j0 = pl.multiple_of(j * tk, tk)
        return _dot(vt_ref[:, pl.ds(j0, tk)], p_ref[...])

    kpos = (jd * tk + lax.broadcasted_iota(jnp.int32, (tk, tq), 0)).astype(F32)
    qpos = (i * tq + lax.broadcasted_iota(jnp.int32, (tk, tq), 1)).astype(F32)
    bias = -c * jnp.abs(qpos - kpos)
    s = scores(jd, 1) + jnp.concatenate([bias, bias], axis=1)
    m0 = jnp.max(s, axis=0, keepdims=True)
    p = jnp.exp2(s - m0)
    l0 = jnp.sum(p, axis=0, keepdims=True)
    pb_scr[...] = p.astype(BF16)

    def tile(t):
        return jnp.where(t >= jd, t + 1, t)

    def kind_of(j):
        return jnp.where(j < jd, 0, 2)

    def half_step(t, s_cur, s_nxt, p_cur, p_prev, carry):
        m_run, l_run, alpha_prev, smax = carry
        j = tile(t)
        smax_nxt = smax
        if s_nxt is not None:
            jn = tile(t + 1)
            s_val = scores(jn, kind_of(jn))
            s_nxt[...] = s_val
            smax_nxt = jnp.max(s_val, axis=0, keepdims=True)
        sigma = jnp.where(j < jd, -1.0, 1.0).astype(F32)
        su = (sigma * c) * (tpos - (j * tk + half).astype(F32))
        m_new = jnp.maximum(m_run, smax + su)
        p = jnp.exp2(s_cur[...] - (m_new - su))
        alpha = jnp.exp2(m_run - m_new)
        l_new = alpha * l_run + jnp.sum(p, axis=0, keepdims=True)
        p_cur[...] = p.astype(BF16)
        jp = jnp.where(t == 0, jd, tile(t - 1))
        acc_scr[...] = acc_scr[...] * alpha_prev + pv(jp, p_prev)
        return m_new, l_new, alpha, smax_nxt

    def pair(pp, carry):
        carry = half_step(2 * pp, sa_scr, sb_scr, pa_scr, pb_scr, carry)
        return half_step(2 * pp + 1, sb_scr, sa_scr, pb_scr, pa_scr, carry)

    j_first = tile(0)
    s_first = scores(j_first, kind_of(j_first))
    sa_scr[...] = s_first
    carry = (m0, l0, jnp.ones((1, wide), F32), jnp.max(s_first, axis=0, keepdims=True))
    npairs = (nk - 2) // 2
    carry = lax.fori_loop(0, npairs, pair, carry, unroll=3 if npairs % 3 == 0 else 1)
    _, l_run, alpha_last, _ = half_step(nk - 2, sa_scr, None, pa_scr, pb_scr, carry)
    acc_scr[...] = acc_scr[...] * alpha_last + pv(tile(nk - 2), pa_scr)

    lp = lp_ref[...]
    lam = (jnp.exp(jnp.sum(lp[0:1] * lp[1:2], axis=1, keepdims=True))
           - jnp.exp(jnp.sum(lp[2:3] * lp[3:4], axis=1, keepdims=True)) + lambda_init)
    o_all = acc_scr[...] / l_run
    o = o_all[:, :tq] - lam * o_all[:, tq:]
    o = o * lax.rsqrt(jnp.mean(o * o, axis=0, keepdims=True) + EPS) * jnp.tile(dn_ref[...], (1, tq // LANES))
    o_ref[0] = (o * (1.0 - lambda_init)).T.astype(o_ref.dtype)


def _attention(dqt, dk, dvt, ctab, lp, dn, lambda_init):
    B, S, _ = dk.shape
    tk = min(512, S // 2)
    tq = min(256, tk)
    assert tk % tq == 0 and tk // 2 <= 256 and (S // tk) % 2 == 0
    nq = S // tq
    return pl.pallas_call(
        functools.partial(_attn_kernel, lambda_init, tk),
        grid=(B, DIFF_HEADS, nq),
        in_specs=[pl.BlockSpec((DIFF_DV, tq), lambda b, h, i: (h, b * nq + i)),
                  pl.BlockSpec((1, S, DIFF_DV), lambda b, h, i: (b, 0, h)),
                  pl.BlockSpec((DIFF_DV, S), lambda b, h, i: (h, b)),
                  pl.BlockSpec((1, SUBLANES, LANES), lambda b, h, i: (h, 0, 0)),
                  pl.BlockSpec((4, DIFF_DH), lambda b, h, i: (0, 0)),
                  pl.BlockSpec((DIFF_DV, LANES), lambda b, h, i: (0, 0))],
        out_specs=pl.BlockSpec((1, tq, DIFF_DV), lambda b, h, i: (b, i, h)),
        out_shape=jax.ShapeDtypeStruct((B, S, DIFF_WIDTH), BF16),
        scratch_shapes=[pltpu.VMEM((3, 2 * LANES, 2 * tq), BF16),
                        pltpu.VMEM((tk, LANES), BF16),
                        pltpu.VMEM((tk, 2 * tq), F32),
                        pltpu.VMEM((tk, 2 * tq), F32),
                        pltpu.VMEM((tk, 2 * tq), BF16),
                        pltpu.VMEM((tk, 2 * tq), BF16),
                        pltpu.VMEM((DIFF_DV, 2 * tq), F32)],
        compiler_params=_params(("parallel", "parallel", "parallel")),
        name="diff_attn",
    )(dqt, dk, dvt, ctab, lp, dn)


def _mix_kernel(x_ref, of_ref, ob_ref, gate_ref, oa_ref, gn_ref, wo_ref, y_ref):
    o = of_ref[...] + ob_ref[...]
    gate = gate_ref[...]
    parts = []
    for h in range(GDN_HEADS):
        cols = slice(h * GDN_DV, (h + 1) * GDN_DV)
        oh = o[:, cols]
        oh = oh * lax.rsqrt(jnp.mean(oh * oh, axis=-1, keepdims=True) + EPS) * gn_ref[...]
        gh = gate[:, cols]
        parts.append(oh * (gh * _sigmoid(gh)))
    og = jnp.concatenate(parts, axis=1).astype(BF16)
    y_ref[...] = (x_ref[...] + _dot(og, wo_ref[0:GDN_WIDTH, :])
                  + _dot(oa_ref[...], wo_ref[GDN_WIDTH:GDN_WIDTH + DIFF_WIDTH, :]))


def _mix(x2d, of, ob, gate, oa, gn, wo):
    T = x2d.shape[0]
    tm = min(512, T)
    row = lambda i: (i, 0)
    fixed = lambda i: (0, 0)
    half = pl.BlockSpec((tm, GDN_WIDTH), row)
    return pl.pallas_call(
        _mix_kernel,
        grid=(T // tm,),
        in_specs=[pl.BlockSpec((tm, D_MODEL), row), half, half, half, half,
                  pl.BlockSpec((1, GDN_DV), fixed),
                  pl.BlockSpec((GDN_WIDTH + DIFF_WIDTH, D_MODEL), fixed)],
        out_specs=pl.BlockSpec((tm, D_MODEL), row),
        out_shape=jax.ShapeDtypeStruct((T, D_MODEL), F32),
        compiler_params=_params(("parallel",)),
        name="mix_out",
    )(x2d, of, ob, gate, oa, gn, wo)


def _mlp_kernel(final, x_ref, nw_ref, wu_ref, wd_ref, nf_ref, y_ref):
    x = x_ref[...]
    ms = jnp.mean(x * x, axis=-1, keepdims=True)
    h = (x * lax.rsqrt(ms + EPS) * nw_ref[...]).astype(BF16)
    acc = x
    step = D_MODEL
    for c in range(D_FF // step):
        up = _dot(h, wu_ref[:, c * step:(c + 1) * step])
        act = jnp.square(jnp.maximum(up, 0.0)).astype(BF16)
        acc = acc + _dot(act, wd_ref[c * step:(c + 1) * step, :])
    if final:
        ms2 = jnp.mean(acc * acc, axis=-1, keepdims=True)
        acc = acc * lax.rsqrt(ms2 + EPS) * nf_ref[...]
    y_ref[...] = acc


def _mlp(x2d, nw, wu, wd, nf, final):
    T = x2d.shape[0]
    tm = min(512, T)
    row = lambda i: (i, 0)
    fixed = lambda i: (0, 0)
    return pl.pallas_call(
        functools.partial(_mlp_kernel, final),
        grid=(T // tm,),
        in_specs=[pl.BlockSpec((tm, D_MODEL), row),
                  pl.BlockSpec((1, D_MODEL), fixed),
                  pl.BlockSpec((D_MODEL, D_FF), fixed),
                  pl.BlockSpec((D_FF, D_MODEL), fixed),
                  pl.BlockSpec((1, D_MODEL), fixed)],
        out_specs=pl.BlockSpec((tm, D_MODEL), row),
        out_shape=jax.ShapeDtypeStruct((T, D_MODEL), F32),
        compiler_params=_params(("parallel",)),
        name="mlp",
    )(x2d, nw, wu, wd, nf)


def _prepare_weights(norm_mix, w_in, conv_w, a_log, dt_bias, gdn_norm, lambda_q1, lambda_k1,
                     lambda_q2, lambda_k2, diff_norm, w_out, norm_mlp, w_up, w_down, norm_final):
    layers = []
    ngate = 2 * GDN_HEADS
    q_lo, k_lo, v_lo = AB_HI, AB_HI + DIFF_WIDTH, AB_HI + 2 * DIFF_WIDTH
    for l in range(DEPTH):
        wm = jnp.concatenate([w_in[l][:, :AB_LO], w_in[l][:, k_lo:v_lo]], axis=1).astype(BF16)
        wab = jnp.pad(w_in[l][:, AB_LO:AB_HI], ((0, 0), (0, LANES - (AB_HI - AB_LO)))).astype(BF16)
        wqt = w_in[l][:, q_lo:k_lo].T.astype(BF16)
        wvt = w_in[l][:, v_lo:v_lo + DIFF_WIDTH].T.astype(BF16)
        gp = jnp.stack([jnp.pad(a_log[l].reshape(ngate).astype(F32), (0, LANES - ngate)),
                        jnp.pad(dt_bias[l].reshape(ngate).astype(F32), (0, LANES - ngate))])
        lp = jnp.stack([lambda_q1[l], lambda_k1[l], lambda_q2[l], lambda_k2[l]]).astype(F32)
        layers.append(dict(
            nmix=norm_mix[l].reshape(1, D_MODEL), wm=wm, wab=wab, wqt=wqt, wvt=wvt,
            cw=conv_w[l], gp=gp, gn=gdn_norm[l].reshape(1, GDN_DV),
            lp=lp, dn=jnp.broadcast_to(diff_norm[l].astype(F32)[:, None], (DIFF_DV, LANES)),
            wo=w_out[l].astype(BF16), nmlp=norm_mlp[l].reshape(1, D_MODEL),
            wu=w_up[l].astype(BF16), wd=w_down[l].astype(BF16)))
    return layers, jnp.asarray(_alibi_table()), norm_final.reshape(1, D_MODEL)


def _bf16_round(x):
    u = np.asarray(x, np.float32).view(np.uint32)
    u = (u + np.uint32(0x7FFF) + ((u >> np.uint32(16)) & np.uint32(1))) & np.uint32(0xFFFF0000)
    return u.view(np.float32)


def _alibi_table():
    tab = np.zeros((DIFF_HEADS, SUBLANES, LANES), np.float32)
    for h in range(DIFF_HEADS):
        c = np.float32(2.0 ** (-8.0 * (h + 1) / DIFF_HEADS) * LOG2E)
        hi = _bf16_round(c)
        mid = _bf16_round(np.float32(c - hi))
        lo = _bf16_round(np.float32(c - hi - mid))
        tab[h, 0], tab[h, 1], tab[h, 2] = hi, mid, lo
        tab[h, 3] = np.float32(np.float32(hi + mid) + lo)
    return tab


def _trunk(x, layers, ctab, nf):
    B, S, _ = x.shape
    T = B * S
    x2d = x.reshape(T, D_MODEL)
    for l, p in enumerate(layers):
        lambda_init = 0.8 - 0.6 * math.exp(-0.3 * l)
        qkv, gate, ab, dqt, dk, dvt = _inproj(x2d, p["nmix"], p["wm"], p["wab"], p["wqt"], p["wvt"])
        r3 = lambda t: t.reshape(B, S, t.shape[-1])
        q, k, v, g = _gdn_prep(r3(qkv), r3(ab), p["cw"], p["gp"])
        u, w, qd, kd, at, gl = _gdn_intra(q, k, v, g)
        of, ob = _gdn_scan(u, w, qd, kd, at, gl)
        oa = _attention(dqt, r3(dk), dvt, ctab, p["lp"], p["dn"], lambda_init)
        x2d = _mix(x2d, of.reshape(T, GDN_WIDTH), ob.reshape(T, GDN_WIDTH), gate,
                   oa.reshape(T, DIFF_WIDTH), p["gn"], p["wo"])
        x2d = _mlp(x2d, p["nmlp"], p["wu"], p["wd"], nf, l == DEPTH - 1)
    return x2d.reshape(B, S, D_MODEL)


def kernel(x_prompt, x_sample, norm_mix, w_in, conv_w, a_log, dt_bias, gdn_norm, lambda_q1,
           lambda_k1, lambda_q2, lambda_k2, diff_norm, w_out, norm_mlp, w_up, w_down, norm_final):
    layers, ctab, nf = _prepare_weights(
        norm_mix, w_in, conv_w, a_log, dt_bias, gdn_norm, lambda_q1, lambda_k1,
        lambda_q2, lambda_k2, diff_norm, w_out, norm_mlp, w_up, w_down, norm_final)
    return (_trunk(x_prompt, layers, ctab, nf), _trunk(x_sample, layers, ctab, nf))
```

```python
import functools
import math

import jax
import jax.numpy as jnp
import numpy as np
from jax import lax
from jax.experimental import pallas as pl
from jax.experimental.pallas import tpu as pltpu

D_MODEL = 1024
DEPTH = 2
GDN_HEADS = 4
GDN_DK = 128
GDN_DV = 128
GDN_WIDTH = GDN_HEADS * GDN_DV
CONV_W = 5
CONV_PAD = CONV_W // 2
CHUNK = 64
DIFF_HEADS = 4
DIFF_DH = 64
DIFF_DV = 2 * DIFF_DH
DIFF_WIDTH = DIFF_HEADS * DIFF_DV
D_FF = 4 * D_MODEL
EPS = 1e-6
QKV_W = 3 * GDN_WIDTH
AB_LO = QKV_W + GDN_WIDTH
AB_HI = AB_LO + 4 * GDN_HEADS
LANES = 128
SUBLANES = 8
VMEM_LIMIT = 56 * 1024 * 1024
NEG_BIG = -1e30
LOG2E = math.log2(math.e)
Q_SCALE = DIFF_DH ** -0.5 * LOG2E
SCAN_UNITS = 64
DENSE_TOKENS = 512
GDN_TOKENS = 256
ATTN_QUERIES = 256
ATTN_KEYS = 512
BF16 = jnp.bfloat16
F32 = jnp.float32


def _dot(a, b):
    return jnp.dot(a, b, preferred_element_type=F32)


def _dot_nt(a, b):
    return lax.dot_general(a, b, (((1,), (1,)), ((), ())), preferred_element_type=F32)


def _dot_tn(a, b):
    return lax.dot_general(a, b, (((0,), (0,)), ((), ())), preferred_element_type=F32)


def _sigmoid(x):
    return 1.0 / (1.0 + jnp.exp(-x))


def _params(sem):
    return pltpu.CompilerParams(dimension_semantics=sem, vmem_limit_bytes=VMEM_LIMIT)


def _inproj_kernel(x_ref, nw_ref, wm_ref, wab_ref, wqt_ref, wvt_ref,
                   qkv_ref, gate_ref, ab_ref, dqt_ref, dk_ref, dvt_ref):
    x = x_ref[...]
    ms = jnp.mean(x * x, axis=-1, keepdims=True)
    h = (x * lax.rsqrt(ms + EPS) * nw_ref[...]).astype(BF16)
    qkv_ref[...] = _dot(h, wm_ref[:, 0:QKV_W])
    gate_ref[...] = _dot(h, wm_ref[:, QKV_W:AB_LO])
    dk_ref[...] = _dot(h, wm_ref[:, AB_LO:AB_LO + DIFF_WIDTH]).astype(BF16)
    ab_ref[...] = _dot(h, wab_ref[...])
    dqt_ref[...] = (_dot_nt(wqt_ref[...], h) * Q_SCALE).astype(BF16)
    dvt_ref[...] = _dot_nt(wvt_ref[...], h).astype(BF16)


def _inproj(x2d, nw, wm, wab, wqt, wvt):
    T = x2d.shape[0]
    tm = min(DENSE_TOKENS, T)
    row = lambda i: (i, 0)
    col = lambda i: (0, i)
    fixed = lambda i: (0, 0)
    nmain = wm.shape[1]
    return pl.pallas_call(
        _inproj_kernel,
        grid=(T // tm,),
        in_specs=[pl.BlockSpec((tm, D_MODEL), row),
                  pl.BlockSpec((1, D_MODEL), fixed),
                  pl.BlockSpec((D_MODEL, nmain), fixed),
                  pl.BlockSpec((D_MODEL, LANES), fixed),
                  pl.BlockSpec((DIFF_WIDTH, D_MODEL), fixed),
                  pl.BlockSpec((DIFF_WIDTH, D_MODEL), fixed)],
        out_specs=[pl.BlockSpec((tm, QKV_W), row),
                   pl.BlockSpec((tm, GDN_WIDTH), row),
                   pl.BlockSpec((tm, LANES), row),
                   pl.BlockSpec((DIFF_WIDTH, tm), col),
                   pl.BlockSpec((tm, DIFF_WIDTH), row),
                   pl.BlockSpec((DIFF_WIDTH, tm), col)],
        out_shape=[jax.ShapeDtypeStruct((T, QKV_W), F32),
                   jax.ShapeDtypeStruct((T, GDN_WIDTH), F32),
                   jax.ShapeDtypeStruct((T, LANES), F32),
                   jax.ShapeDtypeStruct((DIFF_WIDTH, T), BF16),
                   jax.ShapeDtypeStruct((T, DIFF_WIDTH), BF16),
                   jax.ShapeDtypeStruct((DIFF_WIDTH, T), BF16)],
        compiler_params=_params(("parallel",)),
        name="inproj",
    )(x2d, nw, wm, wab, wqt, wvt)


def _gdn_prep_kernel(cur_ref, prev_ref, next_ref, ab_ref, cw_ref, gp_ref,
                     q_ref, k_ref, v_ref, g_ref, xpad):
    i = pl.program_id(1)
    n = pl.num_programs(1)
    tm = cur_ref.shape[1]
    halo = SUBLANES
    xpad[halo:halo + tm, :] = cur_ref[0]
    xpad[0:halo, :] = jnp.where(i > 0, prev_ref[0], 0.0)
    xpad[halo + tm:2 * halo + tm, :] = jnp.where(i < n - 1, next_ref[0], 0.0)

    outs = (q_ref, k_ref, v_ref)
    for grp in range(3):
        c0 = grp * GDN_WIDTH
        acc = None
        for kk in range(CONV_W):
            r0 = halo - CONV_PAD + kk
            term = xpad[r0:r0 + tm, c0:c0 + GDN_WIDTH] * cw_ref[kk:kk + 1, c0:c0 + GDN_WIDTH]
            acc = term if acc is None else acc + term
        y = acc * _sigmoid(acc)
        if grp < 2:
            parts = []
            for h in range(GDN_HEADS):
                yh = y[:, h * GDN_DK:(h + 1) * GDN_DK]
                inv = lax.rsqrt(jnp.sum(yh * yh, axis=-1, keepdims=True) + EPS)
                if grp == 0:
                    inv = inv * (GDN_DK ** -0.5)
                parts.append(yh * inv)
            y = jnp.concatenate(parts, axis=1)
        outs[grp][0] = y

    ab = ab_ref[0]
    a_coef = -jnp.exp(gp_ref[0:1, :])
    z = ab + gp_ref[1:2, :]
    softplus = jnp.maximum(z, 0.0) + jnp.log1p(jnp.exp(-jnp.abs(z)))
    lane = lax.broadcasted_iota(jnp.int32, ab.shape, 1)
    ng = 2 * GDN_HEADS
    g_ref[0] = jnp.where(lane < ng, a_coef * softplus,
                         jnp.where(lane < 2 * ng, _sigmoid(ab), 0.0))


def _gdn_prep(qkv, ab, cw, gp):
    B, S, _ = qkv.shape
    tm = min(GDN_TOKENS, S)
    nb = tm // SUBLANES
    last = S // SUBLANES - 1
    tok = lambda b, i: (b, i, 0)
    fixed = lambda b, i: (0, 0)
    return pl.pallas_call(
        _gdn_prep_kernel,
        grid=(B, S // tm),
        in_specs=[pl.BlockSpec((1, tm, QKV_W), tok),
                  pl.BlockSpec((1, SUBLANES, QKV_W), lambda b, i: (b, jnp.maximum(i * nb - 1, 0), 0)),
                  pl.BlockSpec((1, SUBLANES, QKV_W), lambda b, i: (b, jnp.minimum((i + 1) * nb, last), 0)),
                  pl.BlockSpec((1, tm, LANES), tok),
                  pl.BlockSpec((CONV_W, QKV_W), fixed),
                  pl.BlockSpec((2, LANES), fixed)],
        out_specs=[pl.BlockSpec((1, tm, GDN_WIDTH), tok)] * 3 + [pl.BlockSpec((1, tm, LANES), tok)],
        out_shape=[jax.ShapeDtypeStruct((B, S, GDN_WIDTH), F32)] * 3
                  + [jax.ShapeDtypeStruct((B, S, LANES), F32)],
        scratch_shapes=[pltpu.VMEM((tm + 2 * SUBLANES, QKV_W), F32)],
        compiler_params=_params(("parallel", "parallel")),
        name="gdn_prep",
    )(qkv, qkv, qkv, ab, cw, gp)


def _gdn_intra_kernel(q_ref, k_ref, v_ref, g_ref,
                      u_ref, w_ref, qd_ref, kd_ref, at_ref, gl_ref):
    tc = q_ref.shape[1]
    C = CHUNK
    H = GDN_HEADS
    G = g_ref[0]
    pos = lax.broadcasted_iota(jnp.int32, G.shape, 0) & (C - 1)
    gcf = G
    gcb = G
    s = 1
    while s < C:
        gcf = gcf + jnp.where(pos >= s, pltpu.roll(gcf, s, axis=0), 0.0)
        gcb = gcb + jnp.where(pos < C - s, pltpu.roll(gcb, tc - s, axis=0), 0.0)
        s *= 2

    ri = lax.broadcasted_iota(jnp.int32, (C, 2 * C), 0)
    li = lax.broadcasted_iota(jnp.int32, (C, 2 * C), 1)
    ci = li & (C - 1)
    is_b = li >= C
    is_f = li < C
    incl = (is_f & (ci <= ri)) | (is_b & (ci >= ri))
    strict = (is_f & (ci < ri)) | (is_b & (ci > ri))
    colsum_mask = (is_f & (ri <= ci)) | (is_b & (ri >= ci))
    eye2 = (ri == ci).astype(F32)
    half_f = is_f.astype(BF16)
    half_b = is_b.astype(BF16)

    def block_diag(x):
        return jnp.concatenate([x * half_f, x * half_b], axis=0)

    heads = [(c, h) for c in range(tc // C) for h in range(H)]
    qs, ks, kkqk = {}, {}, {}
    for (c, h) in heads:
        r0 = c * C
        qs[c, h] = q_ref[0, r0:r0 + C, h * GDN_DK:(h + 1) * GDN_DK]
        ks[c, h] = k_ref[0, r0:r0 + C, h * GDN_DK:(h + 1) * GDN_DK]
    for (c, h) in heads:
        kb16 = ks[c, h].astype(BF16)
        kq = jnp.concatenate([kb16, qs[c, h].astype(BF16)], axis=0)
        kkqk[c, h] = _dot_nt(kq, jnp.concatenate([kb16, kb16], axis=0))

    pmat, amat, rhs = {}, {}, {}
    for (c, h) in heads:
        r0 = c * C
        rows = slice(r0, r0 + C)
        kh = ks[c, h]
        vh = v_ref[0, rows, h * GDN_DV:(h + 1) * GDN_DV]
        cols = {}
        for d in range(2):
            ln = d * H + h
            gc_all = gcf if d == 0 else gcb
            gc_col = gc_all[rows, ln:ln + 1]
            beta_col = G[rows, 2 * H + ln:2 * H + ln + 1]
            last = C - 1 if d == 0 else 0
            gc_last = gc_all[r0 + last:r0 + last + 1, ln:ln + 1]
            e_col = jnp.exp(gc_col)
            cols[d] = (G[rows, ln:ln + 1], gc_col, beta_col)
            rhs[c, h, d] = jnp.concatenate([vh * beta_col, kh * (beta_col * e_col)], axis=1).astype(BF16)
            lo = ln * GDN_DK
            qd_ref[0, rows, lo:lo + GDN_DK] = (qs[c, h] * e_col).astype(BF16)
            kd_ref[0, rows, lo:lo + GDN_DK] = (kh * jnp.exp(gc_last - gc_col)).astype(BF16)
            gl_ref[0, c, ln:ln + 1, :] = jnp.broadcast_to(jnp.exp(gc_last), (1, LANES))
        g2, gc2, beta2 = (jnp.where(is_b, cols[1][t], cols[0][t]) for t in range(3))
        gc_row = jnp.sum(jnp.where(colsum_mask, g2, 0.0), axis=0, keepdims=True)
        decay = jnp.exp(jnp.where(incl, gc2 - gc_row, NEG_BIG))
        m1 = kkqk[c, h]
        amat[c, h] = jnp.where(strict, (m1[:C] * beta2) * decay, 0.0)
        at_ref[0, rows, h * 2 * C:(h + 1) * 2 * C] = (m1[C:] * decay).astype(BF16)

    a16 = {}
    for hd in heads:
        pmat[hd] = eye2 - jnp.where((ri >> 1) == (ci >> 1), amat[hd], 0.0)
        a16[hd] = amat[hd].astype(BF16)
    lb = 1
    while (1 << lb) < C:
        coupling = (((ri >> (lb + 1)) == (ci >> (lb + 1))) & ((ri >> lb) != (ci >> lb))).astype(BF16)
        xs, p16 = {}, {}
        for hd in heads:
            p16[hd] = pmat[hd].astype(BF16)
            xs[hd] = _dot(a16[hd] * coupling, block_diag(p16[hd])).astype(BF16)
        for hd in heads:
            pmat[hd] = pmat[hd] - _dot(p16[hd], block_diag(xs[hd]))
        lb += 1

    zero_rhs = jnp.zeros((C, GDN_DV + GDN_DK), BF16)
    for (c, h) in heads:
        rows = slice(c * C, (c + 1) * C)
        both = jnp.concatenate([jnp.concatenate([rhs[c, h, 0], zero_rhs], axis=1),
                                jnp.concatenate([zero_rhs, rhs[c, h, 1]], axis=1)], axis=0)
        sol = _dot(pmat[c, h].astype(BF16), both)
        for d in range(2):
            lo = (d * H + h) * GDN_DK
            so = d * (GDN_DV + GDN_DK)
            u_ref[0, rows, lo:lo + GDN_DV] = sol[:, so:so + GDN_DV]
            w_ref[0, rows, lo:lo + GDN_DK] = sol[:, so + GDN_DV:so + GDN_DV + GDN_DK].astype(BF16)


def _gdn_intra(q, k, v, g):
    B, S, _ = q.shape
    tc = min(GDN_TOKENS, S)
    tok = lambda b, i: (b, i, 0)
    wide = 2 * GDN_WIDTH
    return pl.pallas_call(
        _gdn_intra_kernel,
        grid=(B, S // tc),
        in_specs=[pl.BlockSpec((1, tc, GDN_WIDTH), tok)] * 3 + [pl.BlockSpec((1, tc, LANES), tok)],
        out_specs=[pl.BlockSpec((1, tc, wide), tok)] * 4
                  + [pl.BlockSpec((1, tc, 2 * GDN_HEADS * CHUNK), tok),
                     pl.BlockSpec((1, tc // CHUNK, 2 * GDN_HEADS, LANES), lambda b, i: (b, i, 0, 0))],
        out_shape=[jax.ShapeDtypeStruct((B, S, wide), F32)]
                  + [jax.ShapeDtypeStruct((B, S, wide), BF16)] * 3
                  + [jax.ShapeDtypeStruct((B, S, 2 * GDN_HEADS * CHUNK), BF16),
                     jax.ShapeDtypeStruct((B, S // CHUNK, 2 * GDN_HEADS, LANES), F32)],
        compiler_params=_params(("parallel", "parallel")),
        name="gdn_intra",
    )(q, k, v, g)


def _gdn_scan_kernel(uf_ref, ub_ref, wf_ref, wb_ref, qf_ref, qb_ref, kf_ref, kb_ref,
                     af_ref, ab_ref, gf_ref, gb_ref, of_ref, ob_ref, st_ref):
    n = pl.program_id(0)
    B, tb = uf_ref.shape[0], uf_ref.shape[1]
    C = CHUNK
    H = GDN_HEADS
    nc = tb // C

    @pl.when(n == 0)
    def _():
        st_ref[...] = jnp.zeros_like(st_ref)

    refs = ((uf_ref, wf_ref, qf_ref, kf_ref, af_ref, gf_ref, of_ref),
            (ub_ref, wb_ref, qb_ref, kb_ref, ab_ref, gb_ref, ob_ref))
    chains = [(b, d, h) for b in range(B) for d in range(2) for h in range(H)]
    for cc in range(nc):
        states, ms, vbs = {}, {}, {}
        for ch in chains:
            b, d, h = ch
            u_ref, w_ref, qd_ref, kd_ref, at_ref, gl_ref, o_ref = refs[d]
            r0 = (cc if d == 0 else nc - 1 - cc) * C
            cols = slice(h * GDN_DK, (h + 1) * GDN_DK)
            states[ch] = st_ref[(b * 2 + d) * H + h]
            wq = jnp.concatenate([w_ref[b, r0:r0 + C, cols], qd_ref[b, r0:r0 + C, cols]], axis=0)
            ms[ch] = _dot(wq, states[ch].astype(BF16))
        for ch in chains:
            b, d, h = ch
            u_ref = refs[d][0]
            r0 = (cc if d == 0 else nc - 1 - cc) * C
            cols = slice(h * GDN_DK, (h + 1) * GDN_DK)
            vbs[ch] = (u_ref[b, r0:r0 + C, cols] - ms[ch][:C]).astype(BF16)
        for ch in chains:
            b, d, h = ch
            u_ref, w_ref, qd_ref, kd_ref, at_ref, gl_ref, o_ref = refs[d]
            c = cc if d == 0 else nc - 1 - cc
            r0 = c * C
            ln = d * H + h
            cols = slice(h * GDN_DK, (h + 1) * GDN_DK)
            ac = (2 * h + d) * C
            o_ref[b, r0:r0 + C, cols] = ms[ch][C:] + _dot(at_ref[b, r0:r0 + C, ac:ac + C], vbs[ch])
            gl = gl_ref[b, c, ln:ln + 1, :]
            st_ref[(b * 2 + d) * H + h] = states[ch] * gl + _dot_tn(kd_ref[b, r0:r0 + C, cols], vbs[ch])


def _gdn_scan(u, w, qd, kd, at, gl):
    B, S, _ = u.shape
    tb = min(SCAN_UNITS // (B * 2 * GDN_HEADS) * CHUNK, S)
    nb = S // tb
    nc = tb // CHUNK
    fwd = lambda n: (0, n, 0)
    bwd = lambda n: (0, nb - 1 - n, 1)
    fwd4 = lambda n: (0, n, 0, 0)
    bwd4 = lambda n: (0, nb - 1 - n, 0, 0)
    wide = GDN_WIDTH
    aw = 2 * GDN_HEADS * CHUNK
    pair = lambda shape: [pl.BlockSpec(shape, fwd), pl.BlockSpec(shape, bwd)]
    glshape = (B, nc, 2 * GDN_HEADS, LANES)
    return pl.pallas_call(
        _gdn_scan_kernel,
        grid=(nb,),
        in_specs=pair((B, tb, wide)) * 4
                 + [pl.BlockSpec((B, tb, aw), fwd), pl.BlockSpec((B, tb, aw), lambda n: (0, nb - 1 - n, 0))]
                 + [pl.BlockSpec(glshape, fwd4), pl.BlockSpec(glshape, bwd4)],
        out_specs=[pl.BlockSpec((B, tb, wide), fwd),
                   pl.BlockSpec((B, tb, wide), lambda n: (0, nb - 1 - n, 0))],
        out_shape=[jax.ShapeDtypeStruct((B, S, wide), F32)] * 2,
        scratch_shapes=[pltpu.VMEM((B * 2 * GDN_HEADS, GDN_DK, GDN_DV), F32)],
        compiler_params=_params(("arbitrary",)),
        name="gdn_scan",
    )(u, u, w, w, qd, qd, kd, kd, at, at, gl, gl)


def _attn_kernel(lambda_init, tk, qt_ref, k_ref, vt_ref, ct_ref, lp_ref, dn_ref, o_ref,
                 w_scr, pk_scr, sa_scr, sb_scr, pa_scr, pb_scr, acc_scr):
    i = pl.program_id(2)
    tq = qt_ref.shape[1]
    S = k_ref.shape[1]
    nk = S // tk
    half = tk // 2
    wide = 2 * tq

    qt = qt_ref[...]
    row = lax.broadcasted_iota(jnp.int32, qt.shape, 0)
    zero = jnp.zeros_like(qt)
    top = jnp.concatenate([jnp.where(row < DIFF_DH, qt, zero),
                           jnp.where(row >= DIFF_DH, qt, zero)], axis=1)
    ct = ct_ref[0]
    rep = wide // LANES
    rw = lax.broadcasted_iota(jnp.int32, (LANES, wide), 0)
    pos = jnp.where(rw == 0, jnp.tile(ct[0:1], (1, rep)),
                    jnp.where(rw == 1, jnp.tile(ct[1:2], (1, rep)),
                              jnp.where(rw == 2, jnp.tile(ct[2:3], (1, rep)), 0.0)))
    for kind, sign in enumerate((1.0, 0.0, -1.0)):
        w_scr[kind, 0:LANES, :] = top
        w_scr[kind, LANES:2 * LANES, :] = (sign * pos).astype(BF16)
    kr = lax.broadcasted_iota(jnp.int32, (tk, LANES), 0)
    kl = lax.broadcasted_iota(jnp.int32, (tk, LANES), 1)
    pk_scr[...] = jnp.where(kl < 3, (kr - half).astype(F32), 0.0).astype(BF16)
    acc_scr[...] = jnp.zeros_like(acc_scr)
    c = ct[3:4, 0:1]
    li = lax.broadcasted_iota(jnp.int32, (1, wide), 1)
    tpos = (jnp.where(li >= tq, li - tq, li) + i * tq).astype(F32)
    jd = (i * tq) // tk

    def scores(j, kind):
        j0 = pl.multiple_of(j * tk, tk)
        ka = jnp.concatenate([k_ref[0, pl.ds(j0, tk), :], pk_scr[...]], axis=1)
        return _dot(ka, w_scr[kind])

    def pv(j, p_ref):
        j0 = pl.multiple_of(j * tk, tk)
        return _dot(vt_ref[:, pl.ds(j0, tk)], p_ref[...])

    kpos = (jd * tk + lax.broadcasted_iota(jnp.int32, (tk, tq), 0)).astype(F32)
    qpos = (i * tq + lax.broadcasted_iota(jnp.int32, (tk, tq), 1)).astype(F32)
    bias = -c * jnp.abs(qpos - kpos)
    s = scores(jd, 1) + jnp.concatenate([bias, bias], axis=1)
    m0 = jnp.max(s, axis=0, keepdims=True)
    p = jnp.exp2(s - m0)
    l0 = jnp.sum(p, axis=0, keepdims=True)
    pb_scr[...] = p.astype(BF16)

    def tile(t):
        return jnp.where(t >= jd, t + 1, t)

    def kind_of(j):
        return jnp.where(j < jd, 0, 2)

    def half_step(t, s_cur, s_nxt, p_cur, p_prev, carry):
        m_run, l_run, alpha_prev, smax = carry
        j = tile(t)
        smax_nxt = smax
        if s_nxt is not None:
            jn = tile(t + 1)
            s_val = scores(jn, kind_of(jn))
            s_nxt[...] = s_val
            smax_nxt = jnp.max(s_val, axis=0, keepdims=True)
        sigma = jnp.where(j < jd, -1.0, 1.0).astype(F32)
        su = (sigma * c) * (tpos - (j * tk + half).astype(F32))
        m_new = jnp.maximum(m_run, smax + su)
        p = jnp.exp2(s_cur[...] - (m_new - su))
        alpha = jnp.exp2(m_run - m_new)
        l_new = alpha * l_run + jnp.sum(p, axis=0, keepdims=True)
        p_cur[...] = p.astype(BF16)
        jp = jnp.where(t == 0, jd, tile(t - 1))
        acc_scr[...] = acc_scr[...] * alpha_prev + pv(jp, p_prev)
        return m_new, l_new, alpha, smax_nxt

    def pair(pp, carry):
        carry = half_step(2 * pp, sa_scr, sb_scr, pa_scr, pb_scr, carry)
        return half_step(2 * pp + 1, sb_scr, sa_scr, pb_scr, pa_scr, carry)

    j_first = tile(0)
    s_first = scores(j_first, kind_of(j_first))
    sa_scr[...] = s_first
    carry = (m0, l0, jnp.ones((1, wide), F32), jnp.max(s_first, axis=0, keepdims=True))
    npairs = (nk - 2) // 2
    carry = lax.fori_loop(0, npairs, pair, carry, unroll=3 if npairs % 3 == 0 else 1)
    _, l_run, alpha_last, _ = half_step(nk - 2, sa_scr, None, pa_scr, pb_scr, carry)
    acc_scr[...] = acc_scr[...] * alpha_last + pv(tile(nk - 2), pa_scr)

    lp = lp_ref[...]
    lam = (jnp.exp(jnp.sum(lp[0:1] * lp[1:2], axis=1, keepdims=True))
           - jnp.exp(jnp.sum(lp[2:3] * lp[3:4], axis=1, keepdims=True)) + lambda_init)
    o_all = acc_scr[...] / l_run
    o = o_all[:, :tq] - lam * o_all[:, tq:]
    o = o * lax.rsqrt(jnp.mean(o * o, axis=0, keepdims=True) + EPS) * jnp.tile(dn_ref[...], (1, tq // LANES))
    o_ref[0] = (o * (1.0 - lambda_init)).T.astype(o_ref.dtype)


def _attention(dqt, dk, dvt, ctab, lp, dn, lambda_init):
    B, S, _ = dk.shape
    tk = min(ATTN_KEYS, S // 2)
    tq = min(ATTN_QUERIES, tk)
    assert tk % tq == 0 and tk // 2 <= 256 and (S // tk) % 2 == 0
    nq = S // tq
    return pl.pallas_call(
        functools.partial(_attn_kernel, lambda_init, tk),
        grid=(B, DIFF_HEADS, nq),
        in_specs=[pl.BlockSpec((DIFF_DV, tq), lambda b, h, i: (h, b * nq + i)),
                  pl.BlockSpec((1, S, DIFF_DV), lambda b, h, i: (b, 0, h)),
                  pl.BlockSpec((DIFF_DV, S), lambda b, h, i: (h, b)),
                  pl.BlockSpec((1, SUBLANES, LANES), lambda b, h, i: (h, 0, 0)),
                  pl.BlockSpec((4, DIFF_DH), lambda b, h, i: (0, 0)),
                  pl.BlockSpec((DIFF_DV, LANES), lambda b, h, i: (0, 0))],
        out_specs=pl.BlockSpec((1, tq, DIFF_DV), lambda b, h, i: (b, i, h)),
        out_shape=jax.ShapeDtypeStruct((B, S, DIFF_WIDTH), BF16),
        scratch_shapes=[pltpu.VMEM((3, 2 * LANES, 2 * tq), BF16),
                        pltpu.VMEM((tk, LANES), BF16),
                        pltpu.VMEM((tk, 2 * tq), F32),
                        pltpu.VMEM((tk, 2 * tq), F32),
                        pltpu.VMEM((tk, 2 * tq), BF16),
                        pltpu.VMEM((tk, 2 * tq), BF16),
                        pltpu.VMEM((DIFF_DV, 2 * tq), F32)],
        compiler_params=_params(("parallel", "parallel", "parallel")),
        name="diff_attn",
    )(dqt, dk, dvt, ctab, lp, dn)


def _mix_kernel(x_ref, of_ref, ob_ref, gate_ref, oa_ref, gn_ref, wo_ref, y_ref):
    o = of_ref[...] + ob_ref[...]
    gate = gate_ref[...]
    parts = []
    for h in range(GDN_HEADS):
        cols = slice(h * GDN_DV, (h + 1) * GDN_DV)
        oh = o[:, cols]
        oh = oh * lax.rsqrt(jnp.mean(oh * oh, axis=-1, keepdims=True) + EPS) * gn_ref[...]
        gh = gate[:, cols]
        parts.append(oh * (gh * _sigmoid(gh)))
    og = jnp.concatenate(parts, axis=1).astype(BF16)
    y_ref[...] = (x_ref[...] + _dot(og, wo_ref[0:GDN_WIDTH, :])
                  + _dot(oa_ref[...], wo_ref[GDN_WIDTH:GDN_WIDTH + DIFF_WIDTH, :]))


def _mix(x2d, of, ob, gate, oa, gn, wo):
    T = x2d.shape[0]
    tm = min(DENSE_TOKENS, T)
    row = lambda i: (i, 0)
    fixed = lambda i: (0, 0)
    half = pl.BlockSpec((tm, GDN_WIDTH), row)
    return pl.pallas_call(
        _mix_kernel,
        grid=(T // tm,),
        in_specs=[pl.BlockSpec((tm, D_MODEL), row), half, half, half, half,
                  pl.BlockSpec((1, GDN_DV), fixed),
                  pl.BlockSpec((GDN_WIDTH + DIFF_WIDTH, D_MODEL), fixed)],
        out_specs=pl.BlockSpec((tm, D_MODEL), row),
        out_shape=jax.ShapeDtypeStruct((T, D_MODEL), F32),
        compiler_params=_params(("parallel",)),
        name="mix_out",
    )(x2d, of, ob, gate, oa, gn, wo)


def _mlp_kernel(final, x_ref, nw_ref, wu_ref, wd_ref, nf_ref, y_ref):
    x = x_ref[...]
    ms = jnp.mean(x * x, axis=-1, keepdims=True)
    h = (x * lax.rsqrt(ms + EPS) * nw_ref[...]).astype(BF16)
    acc = x
    step = D_MODEL
    for c in range(D_FF // step):
        up = _dot(h, wu_ref[:, c * step:(c + 1) * step])
        act = jnp.square(jnp.maximum(up, 0.0)).astype(BF16)
        acc = acc + _dot(act, wd_ref[c * step:(c + 1) * step, :])
    if final:
        ms2 = jnp.mean(acc * acc, axis=-1, keepdims=True)
        acc = acc * lax.rsqrt(ms2 + EPS) * nf_ref[...]
    y_ref[...] = acc


def _mlp(x2d, nw, wu, wd, nf, final):
    T = x2d.shape[0]
    tm = min(DENSE_TOKENS, T)
    row = lambda i: (i, 0)
    fixed = lambda i: (0, 0)
    return pl.pallas_call(
        functools.partial(_mlp_kernel, final),
        grid=(T // tm,),
        in_specs=[pl.BlockSpec((tm, D_MODEL), row),
                  pl.BlockSpec((1, D_MODEL), fixed),
                  pl.BlockSpec((D_MODEL, D_FF), fixed),
                  pl.BlockSpec((D_FF, D_MODEL), fixed),
                  pl.BlockSpec((1, D_MODEL), fixed)],
        out_specs=pl.BlockSpec((tm, D_MODEL), row),
        out_shape=jax.ShapeDtypeStruct((T, D_MODEL), F32),
        compiler_params=_params(("parallel",)),
        name="mlp",
    )(x2d, nw, wu, wd, nf)


def _prepare_weights(norm_mix, w_in, conv_w, a_log, dt_bias, gdn_norm, lambda_q1, lambda_k1,
                     lambda_q2, lambda_k2, diff_norm, w_out, norm_mlp, w_up, w_down, norm_final):
    layers = []
    ngate = 2 * GDN_HEADS
    q_lo, k_lo, v_lo = AB_HI, AB_HI + DIFF_WIDTH, AB_HI + 2 * DIFF_WIDTH
    for l in range(DEPTH):
        wm = jnp.concatenate([w_in[l][:, :AB_LO], w_in[l][:, k_lo:v_lo]], axis=1).astype(BF16)
        wab = jnp.pad(w_in[l][:, AB_LO:AB_HI], ((0, 0), (0, LANES - (AB_HI - AB_LO)))).astype(BF16)
        wqt = w_in[l][:, q_lo:k_lo].T.astype(BF16)
        wvt = w_in[l][:, v_lo:v_lo + DIFF_WIDTH].T.astype(BF16)
        gp = jnp.stack([jnp.pad(a_log[l].reshape(ngate).astype(F32), (0, LANES - ngate)),
                        jnp.pad(dt_bias[l].reshape(ngate).astype(F32), (0, LANES - ngate))])
        lp = jnp.stack([lambda_q1[l], lambda_k1[l], lambda_q2[l], lambda_k2[l]]).astype(F32)
        layers.append(dict(
            nmix=norm_mix[l].reshape(1, D_MODEL), wm=wm, wab=wab, wqt=wqt, wvt=wvt,
            cw=conv_w[l], gp=gp, gn=gdn_norm[l].reshape(1, GDN_DV),
            lp=lp, dn=jnp.broadcast_to(diff_norm[l].astype(F32)[:, None], (DIFF_DV, LANES)),
            wo=w_out[l].astype(BF16), nmlp=norm_mlp[l].reshape(1, D_MODEL),
            wu=w_up[l].astype(BF16), wd=w_down[l].astype(BF16)))
    return layers, jnp.asarray(_alibi_table()), norm_final.reshape(1, D_MODEL)


def _bf16_round(x):
    u = np.asarray(x, np.float32).view(np.uint32)
    u = (u + np.uint32(0x7FFF) + ((u >> np.uint32(16)) & np.uint32(1))) & np.uint32(0xFFFF0000)
    return u.view(np.float32)


def _alibi_table():
    tab = np.zeros((DIFF_HEADS, SUBLANES, LANES), np.float32)
    for h in range(DIFF_HEADS):
        c = np.float32(2.0 ** (-8.0 * (h + 1) / DIFF_HEADS) * LOG2E)
        hi = _bf16_round(c)
        mid = _bf16_round(np.float32(c - hi))
        lo = _bf16_round(np.float32(c - hi - mid))
        tab[h, 0], tab[h, 1], tab[h, 2] = hi, mid, lo
        tab[h, 3] = np.float32(np.float32(hi + mid) + lo)
    return tab


def _trunk(x, layers, ctab, nf):
    B, S, _ = x.shape
    T = B * S
    x2d = x.reshape(T, D_MODEL)
    for l, p in enumerate(layers):
        lambda_init = 0.8 - 0.6 * math.exp(-0.3 * l)
        qkv, gate, ab, dqt, dk, dvt = _inproj(x2d, p["nmix"], p["wm"], p["wab"], p["wqt"], p["wvt"])
        r3 = lambda t: t.reshape(B, S, t.shape[-1])
        q, k, v, g = _gdn_prep(r3(qkv), r3(ab), p["cw"], p["gp"])
        u, w, qd, kd, at, gl = _gdn_intra(q, k, v, g)
        of, ob = _gdn_scan(u, w, qd, kd, at, gl)
        oa = _attention(dqt, r3(dk), dvt, ctab, p["lp"], p["dn"], lambda_init)
        x2d = _mix(x2d, of.reshape(T, GDN_WIDTH), ob.reshape(T, GDN_WIDTH), gate,
                   oa.reshape(T, DIFF_WIDTH), p["gn"], p["wo"])
        x2d = _mlp(x2d, p["nmlp"], p["wu"], p["wd"], nf, l == DEPTH - 1)
    return x2d.reshape(B, S, D_MODEL)


def kernel(x_prompt, x_sample, norm_mix, w_in, conv_w, a_log, dt_bias, gdn_norm, lambda_q1,
           lambda_k1, lambda_q2, lambda_k2, diff_norm, w_out, norm_mlp, w_up, w_down, norm_final):
    layers, ctab, nf = _prepare_weights(
        norm_mix, w_in, conv_w, a_log, dt_bias, gdn_norm, lambda_q1, lambda_k1,
        lambda_q2, lambda_k2, diff_norm, w_out, norm_mlp, w_up, w_down, norm_final)
    return (_trunk(x_prompt, layers, ctab, nf), _trunk(x_sample, layers, ctab, nf))
```

```python
import functools
import math

import jax
import jax.numpy as jnp
import numpy as np
from jax import lax
from jax.experimental import pallas as pl
from jax.experimental.pallas import tpu as pltpu

D_MODEL = 1024
DEPTH = 2
GDN_HEADS = 4
GDN_DK = 128
GDN_DV = 128
GDN_WIDTH = GDN_HEADS * GDN_DV
CONV_W = 5
CONV_PAD = CONV_W // 2
CHUNK = 64
DIFF_HEADS = 4
DIFF_DH = 64
DIFF_DV = 2 * DIFF_DH
DIFF_WIDTH = DIFF_HEADS * DIFF_DV
D_FF = 4 * D_MODEL
EPS = 1e-6
QKV_W = 3 * GDN_WIDTH
AB_LO = QKV_W + GDN_WIDTH
AB_HI = AB_LO + 4 * GDN_HEADS
LANES = 128
SUBLANES = 8
VMEM_LIMIT = 56 * 1024 * 1024
NEG_BIG = -1e30
LOG2E = math.log2(math.e)
Q_SCALE = DIFF_DH ** -0.5 * LOG2E
SCAN_UNITS = 64
DENSE_TOKENS = 512
GDN_TOKENS = 256
ATTN_QUERIES = 256
ATTN_KEYS = 512
BF16 = jnp.bfloat16
F32 = jnp.float32


def _dot(a, b):
    return jnp.dot(a, b, preferred_element_type=F32)


def _dot_nt(a, b):
    return lax.dot_general(a, b, (((1,), (1,)), ((), ())), preferred_element_type=F32)


def _dot_tn(a, b):
    return lax.dot_general(a, b, (((0,), (0,)), ((), ())), preferred_element_type=F32)


def _sigmoid(x):
    return 1.0 / (1.0 + jnp.exp(-x))


def _params(sem):
    return pltpu.CompilerParams(dimension_semantics=sem, vmem_limit_bytes=VMEM_LIMIT)


def _inproj_kernel(x_ref, nw_ref, wm_ref, wab_ref, wqt_ref, wvt_ref,
                   qkv_ref, gate_ref, ab_ref, dqt_ref, dk_ref, dvt_ref):
    x = x_ref[...]
    ms = jnp.mean(x * x, axis=-1, keepdims=True)
    h = (x * lax.rsqrt(ms + EPS) * nw_ref[...]).astype(BF16)
    qkv_ref[...] = _dot(h, wm_ref[:, 0:QKV_W])
    gate_ref[...] = _dot(h, wm_ref[:, QKV_W:AB_LO])
    dk_ref[...] = _dot(h, wm_ref[:, AB_LO:AB_LO + DIFF_WIDTH]).astype(BF16)
    ab_ref[...] = _dot(h, wab_ref[...])
    dqt_ref[...] = (_dot_nt(wqt_ref[...], h) * Q_SCALE).astype(BF16)
    dvt_ref[...] = _dot_nt(wvt_ref[...], h).astype(BF16)


def _inproj(x2d, nw, wm, wab, wqt, wvt):
    T = x2d.shape[0]
    tm = min(DENSE_TOKENS, T)
    row = lambda i: (i, 0)
    col = lambda i: (0, i)
    fixed = lambda i: (0, 0)
    nmain = wm.shape[1]
    return pl.pallas_call(
        _inproj_kernel,
        grid=(T // tm,),
        in_specs=[pl.BlockSpec((tm, D_MODEL), row),
                  pl.BlockSpec((1, D_MODEL), fixed),
                  pl.BlockSpec((D_MODEL, nmain), fixed),
                  pl.BlockSpec((D_MODEL, LANES), fixed),
                  pl.BlockSpec((DIFF_WIDTH, D_MODEL), fixed),
                  pl.BlockSpec((DIFF_WIDTH, D_MODEL), fixed)],
        out_specs=[pl.BlockSpec((tm, QKV_W), row),
                   pl.BlockSpec((tm, GDN_WIDTH), row),
                   pl.BlockSpec((tm, LANES), row),
                   pl.BlockSpec((DIFF_WIDTH, tm), col),
                   pl.BlockSpec((tm, DIFF_WIDTH), row),
                   pl.BlockSpec((DIFF_WIDTH, tm), col)],
        out_shape=[jax.ShapeDtypeStruct((T, QKV_W), F32),
                   jax.ShapeDtypeStruct((T, GDN_WIDTH), F32),
                   jax.ShapeDtypeStruct((T, LANES), F32),
                   jax.ShapeDtypeStruct((DIFF_WIDTH, T), BF16),
                   jax.ShapeDtypeStruct((T, DIFF_WIDTH), BF16),
                   jax.ShapeDtypeStruct((DIFF_WIDTH, T), BF16)],
        compiler_params=_params(("parallel",)),
        name="inproj",
    )(x2d, nw, wm, wab, wqt, wvt)


def _gdn_prep_kernel(cur_ref, prev_ref, next_ref, ab_ref, cw_ref, gp_ref,
                     q_ref, k_ref, v_ref, g_ref, xpad):
    i = pl.program_id(1)
    n = pl.num_programs(1)
    tm = cur_ref.shape[1]
    halo = SUBLANES
    xpad[halo:halo + tm, :] = cur_ref[0]
    xpad[0:halo, :] = jnp.where(i > 0, prev_ref[0], 0.0)
    xpad[halo + tm:2 * halo + tm, :] = jnp.where(i < n - 1, next_ref[0], 0.0)

    outs = (q_ref, k_ref, v_ref)
    for grp in range(3):
        c0 = grp * GDN_WIDTH
        acc = None
        for kk in range(CONV_W):
            r0 = halo - CONV_PAD + kk
            term = xpad[r0:r0 + tm, c0:c0 + GDN_WIDTH] * cw_ref[kk:kk + 1, c0:c0 + GDN_WIDTH]
            acc = term if acc is None else acc + term
        y = acc * _sigmoid(acc)
        if grp < 2:
            parts = []
            for h in range(GDN_HEADS):
                yh = y[:, h * GDN_DK:(h + 1) * GDN_DK]
                inv = lax.rsqrt(jnp.sum(yh * yh, axis=-1, keepdims=True) + EPS)
                if grp == 0:
                    inv = inv * (GDN_DK ** -0.5)
                parts.append(yh * inv)
            y = jnp.concatenate(parts, axis=1)
        outs[grp][0] = y

    ab = ab_ref[0]
    a_coef = -jnp.exp(gp_ref[0:1, :])
    z = ab + gp_ref[1:2, :]
    softplus = jnp.maximum(z, 0.0) + jnp.log1p(jnp.exp(-jnp.abs(z)))
    lane = lax.broadcasted_iota(jnp.int32, ab.shape, 1)
    ng = 2 * GDN_HEADS
    g_ref[0] = jnp.where(lane < ng, a_coef * softplus,
                         jnp.where(lane < 2 * ng, _sigmoid(ab), 0.0))


def _gdn_prep(qkv, ab, cw, gp):
    B, S, _ = qkv.shape
    tm = min(GDN_TOKENS, S)
    nb = tm // SUBLANES
    last = S // SUBLANES - 1
    tok = lambda b, i: (b, i, 0)
    fixed = lambda b, i: (0, 0)
    return pl.pallas_call(
        _gdn_prep_kernel,
        grid=(B, S // tm),
        in_specs=[pl.BlockSpec((1, tm, QKV_W), tok),
                  pl.BlockSpec((1, SUBLANES, QKV_W), lambda b, i: (b, jnp.maximum(i * nb - 1, 0), 0)),
                  pl.BlockSpec((1, SUBLANES, QKV_W), lambda b, i: (b, jnp.minimum((i + 1) * nb, last), 0)),
                  pl.BlockSpec((1, tm, LANES), tok),
                  pl.BlockSpec((CONV_W, QKV_W), fixed),
                  pl.BlockSpec((2, LANES), fixed)],
        out_specs=[pl.BlockSpec((1, tm, GDN_WIDTH), tok)] * 3 + [pl.BlockSpec((1, tm, LANES), tok)],
        out_shape=[jax.ShapeDtypeStruct((B, S, GDN_WIDTH), F32)] * 3
                  + [jax.ShapeDtypeStruct((B, S, LANES), F32)],
        scratch_shapes=[pltpu.VMEM((tm + 2 * SUBLANES, QKV_W), F32)],
        compiler_params=_params(("parallel", "parallel")),
        name="gdn_prep",
    )(qkv, qkv, qkv, ab, cw, gp)


def _gdn_intra_kernel(q_ref, k_ref, v_ref, g_ref,
                      u_ref, w_ref, qd_ref, kd_ref, at_ref, gl_ref):
    tc = q_ref.shape[1]
    C = CHUNK
    H = GDN_HEADS
    G = g_ref[0]
    pos = lax.broadcasted_iota(jnp.int32, G.shape, 0) & (C - 1)
    gcf = G
    gcb = G
    s = 1
    while s < C:
        gcf = gcf + jnp.where(pos >= s, pltpu.roll(gcf, s, axis=0), 0.0)
        gcb = gcb + jnp.where(pos < C - s, pltpu.roll(gcb, tc - s, axis=0), 0.0)
        s *= 2

    ri = lax.broadcasted_iota(jnp.int32, (C, 2 * C), 0)
    li = lax.broadcasted_iota(jnp.int32, (C, 2 * C), 1)
    ci = li & (C - 1)
    is_b = li >= C
    is_f = li < C
    incl = (is_f & (ci <= ri)) | (is_b & (ci >= ri))
    strict = (is_f & (ci < ri)) | (is_b & (ci > ri))
    colsum_mask = (is_f & (ri <= ci)) | (is_b & (ri >= ci))
    eye2 = (ri == ci).astype(F32)
    half_f = is_f.astype(BF16)
    half_b = is_b.astype(BF16)

    def block_diag(x):
        return jnp.concatenate([x * half_f, x * half_b], axis=0)

    heads = [(c, h) for c in range(tc // C) for h in range(H)]
    qs, ks, kkqk = {}, {}, {}
    for (c, h) in heads:
        r0 = c * C
        qs[c, h] = q_ref[0, r0:r0 + C, h * GDN_DK:(h + 1) * GDN_DK]
        ks[c, h] = k_ref[0, r0:r0 + C, h * GDN_DK:(h + 1) * GDN_DK]
    for (c, h) in heads:
        kb16 = ks[c, h].astype(BF16)
        kq = jnp.concatenate([kb16, qs[c, h].astype(BF16)], axis=0)
        kkqk[c, h] = _dot_nt(kq, jnp.concatenate([kb16, kb16], axis=0))

    pmat, amat, rhs = {}, {}, {}
    for (c, h) in heads:
        r0 = c * C
        rows = slice(r0, r0 + C)
        kh = ks[c, h]
        vh = v_ref[0, rows, h * GDN_DV:(h + 1) * GDN_DV]
        cols = {}
        for d in range(2):
            ln = d * H + h
            gc_all = gcf if d == 0 else gcb
            gc_col = gc_all[rows, ln:ln + 1]
            beta_col = G[rows, 2 * H + ln:2 * H + ln + 1]
            last = C - 1 if d == 0 else 0
            gc_last = gc_all[r0 + last:r0 + last + 1, ln:ln + 1]
            e_col = jnp.exp(gc_col)
            cols[d] = (G[rows, ln:ln + 1], gc_col, beta_col)
            rhs[c, h, d] = jnp.concatenate([vh * beta_col, kh * (beta_col * e_col)], axis=1).astype(BF16)
            lo = ln * GDN_DK
            qd_ref[0, rows, lo:lo + GDN_DK] = (qs[c, h] * e_col).astype(BF16)
            kd_ref[0, rows, lo:lo + GDN_DK] = (kh * jnp.exp(gc_last - gc_col)).astype(BF16)
            gl_ref[0, c, ln:ln + 1, :] = jnp.broadcast_to(jnp.exp(gc_last), (1, LANES))
        g2, gc2, beta2 = (jnp.where(is_b, cols[1][t], cols[0][t]) for t in range(3))
        gc_row = jnp.sum(jnp.where(colsum_mask, g2, 0.0), axis=0, keepdims=True)
        decay = jnp.exp(jnp.where(incl, gc2 - gc_row, NEG_BIG))
        m1 = kkqk[c, h]
        amat[c, h] = jnp.where(strict, (m1[:C] * beta2) * decay, 0.0)
        at_ref[0, rows, h * 2 * C:(h + 1) * 2 * C] = (m1[C:] * decay).astype(BF16)

    a16 = {}
    for hd in heads:
        pmat[hd] = eye2 - jnp.where((ri >> 1) == (ci >> 1), amat[hd], 0.0)
        a16[hd] = amat[hd].astype(BF16)
    lb = 1
    while (1 << lb) < C:
        coupling = (((ri >> (lb + 1)) == (ci >> (lb + 1))) & ((ri >> lb) != (ci >> lb))).astype(BF16)
        xs, p16 = {}, {}
        for hd in heads:
            p16[hd] = pmat[hd].astype(BF16)
            xs[hd] = _dot(a16[hd] * coupling, block_diag(p16[hd])).astype(BF16)
        for hd in heads:
            pmat[hd] = pmat[hd] - _dot(p16[hd], block_diag(xs[hd]))
        lb += 1

    zero_rhs = jnp.zeros((C, GDN_DV + GDN_DK), BF16)
    for (c, h) in heads:
        rows = slice(c * C, (c + 1) * C)
        both = jnp.concatenate([jnp.concatenate([rhs[c, h, 0], zero_rhs], axis=1),
                                jnp.concatenate([zero_rhs, rhs[c, h, 1]], axis=1)], axis=0)
        sol = _dot(pmat[c, h].astype(BF16), both)
        for d in range(2):
            lo = (d * H + h) * GDN_DK
            so = d * (GDN_DV + GDN_DK)
            u_ref[0, rows, lo:lo + GDN_DV] = sol[:, so:so + GDN_DV].astype(BF16)
            w_ref[0, rows, lo:lo + GDN_DK] = sol[:, so + GDN_DV:so + GDN_DV + GDN_DK].astype(BF16)


def _gdn_intra(q, k, v, g):
    B, S, _ = q.shape
    tc = min(GDN_TOKENS, S)
    tok = lambda b, i: (b, i, 0)
    wide = 2 * GDN_WIDTH
    return pl.pallas_call(
        _gdn_intra_kernel,
        grid=(B, S // tc),
        in_specs=[pl.BlockSpec((1, tc, GDN_WIDTH), tok)] * 3 + [pl.BlockSpec((1, tc, LANES), tok)],
        out_specs=[pl.BlockSpec((1, tc, wide), tok)] * 4
                  + [pl.BlockSpec((1, tc, 2 * GDN_HEADS * CHUNK), tok),
                     pl.BlockSpec((1, tc // CHUNK, 2 * GDN_HEADS, LANES), lambda b, i: (b, i, 0, 0))],
        out_shape=[jax.ShapeDtypeStruct((B, S, wide), BF16)] * 4
                  + [jax.ShapeDtypeStruct((B, S, 2 * GDN_HEADS * CHUNK), BF16),
                     jax.ShapeDtypeStruct((B, S // CHUNK, 2 * GDN_HEADS, LANES), F32)],
        compiler_params=_params(("parallel", "parallel")),
        name="gdn_intra",
    )(q, k, v, g)


def _gdn_scan_kernel(uf_ref, ub_ref, wf_ref, wb_ref, qf_ref, qb_ref, kf_ref, kb_ref,
                     af_ref, ab_ref, gf_ref, gb_ref, of_ref, ob_ref, st_ref):
    n = pl.program_id(0)
    B, tb = uf_ref.shape[0], uf_ref.shape[1]
    C = CHUNK
    H = GDN_HEADS
    nc = tb // C

    @pl.when(n == 0)
    def _():
        st_ref[...] = jnp.zeros_like(st_ref)

    refs = ((uf_ref, wf_ref, qf_ref, kf_ref, af_ref, gf_ref, of_ref),
            (ub_ref, wb_ref, qb_ref, kb_ref, ab_ref, gb_ref, ob_ref))
    chains = [(b, d, h) for b in range(B) for d in range(2) for h in range(H)]
    for cc in range(nc):
        states, ms, vbs = {}, {}, {}
        for ch in chains:
            b, d, h = ch
            u_ref, w_ref, qd_ref, kd_ref, at_ref, gl_ref, o_ref = refs[d]
            r0 = (cc if d == 0 else nc - 1 - cc) * C
            cols = slice(h * GDN_DK, (h + 1) * GDN_DK)
            states[ch] = st_ref[(b * 2 + d) * H + h]
            wq = jnp.concatenate([w_ref[b, r0:r0 + C, cols], qd_ref[b, r0:r0 + C, cols]], axis=0)
            ms[ch] = _dot(wq, states[ch].astype(BF16))
        for ch in chains:
            b, d, h = ch
            u_ref = refs[d][0]
            r0 = (cc if d == 0 else nc - 1 - cc) * C
            cols = slice(h * GDN_DK, (h + 1) * GDN_DK)
            vbs[ch] = (u_ref[b, r0:r0 + C, cols] - ms[ch][:C]).astype(BF16)
        for ch in chains:
            b, d, h = ch
            u_ref, w_ref, qd_ref, kd_ref, at_ref, gl_ref, o_ref = refs[d]
            c = cc if d == 0 else nc - 1 - cc
            r0 = c * C
            ln = d * H + h
            cols = slice(h * GDN_DK, (h + 1) * GDN_DK)
            ac = (2 * h + d) * C
            o = ms[ch][C:] + _dot(at_ref[b, r0:r0 + C, ac:ac + C], vbs[ch])
            o_ref[b, r0:r0 + C, cols] = o.astype(o_ref.dtype)
            gl = gl_ref[b, c, ln:ln + 1, :]
            st_ref[(b * 2 + d) * H + h] = states[ch] * gl + _dot_tn(kd_ref[b, r0:r0 + C, cols], vbs[ch])


def _gdn_scan(u, w, qd, kd, at, gl):
    B, S, _ = u.shape
    tb = min(SCAN_UNITS // (B * 2 * GDN_HEADS) * CHUNK, S)
    nb = S // tb
    nc = tb // CHUNK
    fwd = lambda n: (0, n, 0)
    bwd = lambda n: (0, nb - 1 - n, 1)
    fwd4 = lambda n: (0, n, 0, 0)
    bwd4 = lambda n: (0, nb - 1 - n, 0, 0)
    wide = GDN_WIDTH
    aw = 2 * GDN_HEADS * CHUNK
    pair = lambda shape: [pl.BlockSpec(shape, fwd), pl.BlockSpec(shape, bwd)]
    glshape = (B, nc, 2 * GDN_HEADS, LANES)
    return pl.pallas_call(
        _gdn_scan_kernel,
        grid=(nb,),
        in_specs=pair((B, tb, wide)) * 4
                 + [pl.BlockSpec((B, tb, aw), fwd), pl.BlockSpec((B, tb, aw), lambda n: (0, nb - 1 - n, 0))]
                 + [pl.BlockSpec(glshape, fwd4), pl.BlockSpec(glshape, bwd4)],
        out_specs=[pl.BlockSpec((B, tb, wide), fwd),
                   pl.BlockSpec((B, tb, wide), lambda n: (0, nb - 1 - n, 0))],
        out_shape=[jax.ShapeDtypeStruct((B, S, wide), BF16)] * 2,
        scratch_shapes=[pltpu.VMEM((B * 2 * GDN_HEADS, GDN_DK, GDN_DV), F32)],
        compiler_params=_params(("arbitrary",)),
        name="gdn_scan",
    )(u, u, w, w, qd, qd, kd, kd, at, at, gl, gl)


def _attn_kernel(lambda_init, tk, qt_ref, k_ref, vt_ref, ct_ref, lp_ref, dn_ref, o_ref,
                 w_scr, pk_scr, sa_scr, sb_scr, pa_scr, pb_scr, acc_scr):
    i = pl.program_id(2)
    tq = qt_ref.shape[1]
    S = k_ref.shape[1]
    nk = S // tk
    half = tk // 2
    wide = 2 * tq

    qt = qt_ref[...]
    row = lax.broadcasted_iota(jnp.int32, qt.shape, 0)
    zero = jnp.zeros_like(qt)
    top = jnp.concatenate([jnp.where(row < DIFF_DH, qt, zero),
                           jnp.where(row >= DIFF_DH, qt, zero)], axis=1)
    ct = ct_ref[0]
    rep = wide // LANES
    rw = lax.broadcasted_iota(jnp.int32, (LANES, wide), 0)
    pos = jnp.where(rw == 0, jnp.tile(ct[0:1], (1, rep)),
                    jnp.where(rw == 1, jnp.tile(ct[1:2], (1, rep)),
                              jnp.where(rw == 2, jnp.tile(ct[2:3], (1, rep)), 0.0)))
    for kind, sign in enumerate((1.0, 0.0, -1.0)):
        w_scr[kind, 0:LANES, :] = top
        w_scr[kind, LANES:2 * LANES, :] = (sign * pos).astype(BF16)
    kr = lax.broadcasted_iota(jnp.int32, (tk, LANES), 0)
    kl = lax.broadcasted_iota(jnp.int32, (tk, LANES), 1)
    pk_scr[...] = jnp.where(kl < 3, (kr - half).astype(F32), 0.0).astype(BF16)
    acc_scr[...] = jnp.zeros_like(acc_scr)
    c = ct[3:4, 0:1]
    li = lax.broadcasted_iota(jnp.int32, (1, wide), 1)
    tpos = (jnp.where(li >= tq, li - tq, li) + i * tq).astype(F32)
    jd = (i * tq) // tk

    def scores(j, kind):
        j0 = pl.multiple_of(j * tk, tk)
        ka = jnp.concatenate([k_ref[0, pl.ds(j0, tk), :], pk_scr[...]], axis=1)
        return _dot(ka, w_scr[kind])

    def pv(j, p_ref):
        j0 = pl.multiple_of(j * tk, tk)
        return _dot(vt_ref[:, pl.ds(j0, tk)], p_ref[...])

    kpos = (jd * tk + lax.broadcasted_iota(jnp.int32, (tk, tq), 0)).astype(F32)
    qpos = (i * tq + lax.broadcasted_iota(jnp.int32, (tk, tq), 1)).astype(F32)
    bias = -c * jnp.abs(qpos - kpos)
    s = scores(jd, 1) + jnp.concatenate([bias, bias], axis=1)
    m0 = jnp.max(s, axis=0, keepdims=True)
    p = jnp.exp2(s - m0)
    l0 = jnp.sum(p, axis=0, keepdims=True)
    pb_scr[...] = p.astype(BF16)

    def tile(t):
        return jnp.where(t >= jd, t + 1, t)

    def kind_of(j):
        return jnp.where(j < jd, 0, 2)

    def half_step(t, s_cur, s_nxt, p_cur, p_prev, carry):
        m_run, l_run, alpha_prev, smax = carry
        j = tile(t)
        smax_nxt = smax
        if s_nxt is not None:
            jn = tile(t + 1)
            s_val = scores(jn, kind_of(jn))
            s_nxt[...] = s_val
            smax_nxt = jnp.max(s_val, axis=0, keepdims=True)
        sigma = jnp.where(j < jd, -1.0, 1.0).astype(F32)
        su = (sigma * c) * (tpos - (j * tk + half).astype(F32))
        m_new = jnp.maximum(m_run, smax + su)
        p = jnp.exp2(s_cur[...] - (m_new - su))
        alpha = jnp.exp2(m_run - m_new)
        l_new = alpha * l_run + jnp.sum(p, axis=0, keepdims=True)
        p_cur[...] = p.astype(BF16)
        jp = jnp.where(t == 0, jd, tile(t - 1))
        acc_scr[...] = acc_scr[...] * alpha_prev + pv(jp, p_prev)
        return m_new, l_new, alpha, smax_nxt

    def pair(pp, carry):
        carry = half_step(2 * pp, sa_scr, sb_scr, pa_scr, pb_scr, carry)
        return half_step(2 * pp + 1, sb_scr, sa_scr, pb_scr, pa_scr, carry)

    j_first = tile(0)
    s_first = scores(j_first, kind_of(j_first))
    sa_scr[...] = s_first
    carry = (m0, l0, jnp.ones((1, wide), F32), jnp.max(s_first, axis=0, keepdims=True))
    npairs = (nk - 2) // 2
    carry = lax.fori_loop(0, npairs, pair, carry, unroll=3 if npairs % 3 == 0 else 1)
    _, l_run, alpha_last, _ = half_step(nk - 2, sa_scr, None, pa_scr, pb_scr, carry)
    acc_scr[...] = acc_scr[...] * alpha_last + pv(tile(nk - 2), pa_scr)

    lp = lp_ref[...]
    lam = (jnp.exp(jnp.sum(lp[0:1] * lp[1:2], axis=1, keepdims=True))
           - jnp.exp(jnp.sum(lp[2:3] * lp[3:4], axis=1, keepdims=True)) + lambda_init)
    o_all = acc_scr[...] / l_run
    o = o_all[:, :tq] - lam * o_all[:, tq:]
    o = o * lax.rsqrt(jnp.mean(o * o, axis=0, keepdims=True) + EPS) * jnp.tile(dn_ref[...], (1, tq // LANES))
    o_ref[0] = (o * (1.0 - lambda_init)).T.astype(o_ref.dtype)


def _attention(dqt, dk, dvt, ctab, lp, dn, lambda_init):
    B, S, _ = dk.shape
    tk = min(ATTN_KEYS, S // 2)
    tq = min(ATTN_QUERIES, tk)
    assert tk % tq == 0 and tk // 2 <= 256 and (S // tk) % 2 == 0
    nq = S // tq
    return pl.pallas_call(
        functools.partial(_attn_kernel, lambda_init, tk),
        grid=(B, DIFF_HEADS, nq),
        in_specs=[pl.BlockSpec((DIFF_DV, tq), lambda b, h, i: (h, b * nq + i)),
                  pl.BlockSpec((1, S, DIFF_DV), lambda b, h, i: (b, 0, h)),
                  pl.BlockSpec((DIFF_DV, S), lambda b, h, i: (h, b)),
                  pl.BlockSpec((1, SUBLANES, LANES), lambda b, h, i: (h, 0, 0)),
                  pl.BlockSpec((4, DIFF_DH), lambda b, h, i: (0, 0)),
                  pl.BlockSpec((DIFF_DV, LANES), lambda b, h, i: (0, 0))],
        out_specs=pl.BlockSpec((1, tq, DIFF_DV), lambda b, h, i: (b, i, h)),
        out_shape=jax.ShapeDtypeStruct((B, S, DIFF_WIDTH), BF16),
        scratch_shapes=[pltpu.VMEM((3, 2 * LANES, 2 * tq), BF16),
                        pltpu.VMEM((tk, LANES), BF16),
                        pltpu.VMEM((tk, 2 * tq), F32),
                        pltpu.VMEM((tk, 2 * tq), F32),
                        pltpu.VMEM((tk, 2 * tq), BF16),
                        pltpu.VMEM((tk, 2 * tq), BF16),
                        pltpu.VMEM((DIFF_DV, 2 * tq), F32)],
        compiler_params=_params(("parallel", "parallel", "parallel")),
        name="diff_attn",
    )(dqt, dk, dvt, ctab, lp, dn)


def _mix_kernel(x_ref, of_ref, ob_ref, gate_ref, oa_ref, gn_ref, wo_ref, y_ref):
    o = of_ref[...].astype(F32) + ob_ref[...].astype(F32)
    gate = gate_ref[...]
    parts = []
    for h in range(GDN_HEADS):
        cols = slice(h * GDN_DV, (h + 1) * GDN_DV)
        oh = o[:, cols]
        oh = oh * lax.rsqrt(jnp.mean(oh * oh, axis=-1, keepdims=True) + EPS) * gn_ref[...]
        gh = gate[:, cols]
        parts.append(oh * (gh * _sigmoid(gh)))
    og = jnp.concatenate(parts, axis=1).astype(BF16)
    y_ref[...] = (x_ref[...] + _dot(og, wo_ref[0:GDN_WIDTH, :])
                  + _dot(oa_ref[...], wo_ref[GDN_WIDTH:GDN_WIDTH + DIFF_WIDTH, :]))


def _mix(x2d, of, ob, gate, oa, gn, wo):
    T = x2d.shape[0]
    tm = min(DENSE_TOKENS, T)
    row = lambda i: (i, 0)
    fixed = lambda i: (0, 0)
    half = pl.BlockSpec((tm, GDN_WIDTH), row)
    return pl.pallas_call(
        _mix_kernel,
        grid=(T // tm,),
        in_specs=[pl.BlockSpec((tm, D_MODEL), row), half, half, half, half,
                  pl.BlockSpec((1, GDN_DV), fixed),
                  pl.BlockSpec((GDN_WIDTH + DIFF_WIDTH, D_MODEL), fixed)],
        out_specs=pl.BlockSpec((tm, D_MODEL), row),
        out_shape=jax.ShapeDtypeStruct((T, D_MODEL), F32),
        compiler_params=_params(("parallel",)),
        name="mix_out",
    )(x2d, of, ob, gate, oa, gn, wo)


def _mlp_kernel(final, x_ref, nw_ref, wu_ref, wd_ref, nf_ref, y_ref):
    x = x_ref[...]
    ms = jnp.mean(x * x, axis=-1, keepdims=True)
    h = (x * lax.rsqrt(ms + EPS) * nw_ref[...]).astype(BF16)
    acc = x
    step = D_MODEL
    for c in range(D_FF // step):
        up = _dot(h, wu_ref[:, c * step:(c + 1) * step])
        act = jnp.square(jnp.maximum(up, 0.0)).astype(BF16)
        acc = acc + _dot(act, wd_ref[c * step:(c + 1) * step, :])
    if final:
        ms2 = jnp.mean(acc * acc, axis=-1, keepdims=True)
        acc = acc * lax.rsqrt(ms2 + EPS) * nf_ref[...]
    y_ref[...] = acc


def _mlp(x2d, nw, wu, wd, nf, final):
    T = x2d.shape[0]
    tm = min(DENSE_TOKENS, T)
    row = lambda i: (i, 0)
    fixed = lambda i: (0, 0)
    return pl.pallas_call(
        functools.partial(_mlp_kernel, final),
        grid=(T // tm,),
        in_specs=[pl.BlockSpec((tm, D_MODEL), row),
                  pl.BlockSpec((1, D_MODEL), fixed),
                  pl.BlockSpec((D_MODEL, D_FF), fixed),
                  pl.BlockSpec((D_FF, D_MODEL), fixed),
                  pl.BlockSpec((1, D_MODEL), fixed)],
        out_specs=pl.BlockSpec((tm, D_MODEL), row),
        out_shape=jax.ShapeDtypeStruct((T, D_MODEL), F32),
        compiler_params=_params(("parallel",)),
        name="mlp",
    )(x2d, nw, wu, wd, nf)


def _prepare_weights(norm_mix, w_in, conv_w, a_log, dt_bias, gdn_norm, lambda_q1, lambda_k1,
                     lambda_q2, lambda_k2, diff_norm, w_out, norm_mlp, w_up, w_down, norm_final):
    layers = []
    ngate = 2 * GDN_HEADS
    q_lo, k_lo, v_lo = AB_HI, AB_HI + DIFF_WIDTH, AB_HI + 2 * DIFF_WIDTH
    for l in range(DEPTH):
        wm = jnp.concatenate([w_in[l][:, :AB_LO], w_in[l][:, k_lo:v_lo]], axis=1).astype(BF16)
        wab = jnp.pad(w_in[l][:, AB_LO:AB_HI], ((0, 0), (0, LANES - (AB_HI - AB_LO)))).astype(BF16)
        wqt = w_in[l][:, q_lo:k_lo].T.astype(BF16)
        wvt = w_in[l][:, v_lo:v_lo + DIFF_WIDTH].T.astype(BF16)
        gp = jnp.stack([jnp.pad(a_log[l].reshape(ngate).astype(F32), (0, LANES - ngate)),
                        jnp.pad(dt_bias[l].reshape(ngate).astype(F32), (0, LANES - ngate))])
        lp = jnp.stack([lambda_q1[l], lambda_k1[l], lambda_q2[l], lambda_k2[l]]).astype(F32)
        layers.append(dict(
            nmix=norm_mix[l].reshape(1, D_MODEL), wm=wm, wab=wab, wqt=wqt, wvt=wvt,
            cw=conv_w[l], gp=gp, gn=gdn_norm[l].reshape(1, GDN_DV),
            lp=lp, dn=jnp.broadcast_to(diff_norm[l].astype(F32)[:, None], (DIFF_DV, LANES)),
            wo=w_out[l].astype(BF16), nmlp=norm_mlp[l].reshape(1, D_MODEL),
            wu=w_up[l].astype(BF16), wd=w_down[l].astype(BF16)))
    return layers, jnp.asarray(_alibi_table()), norm_final.reshape(1, D_MODEL)


def _bf16_round(x):
    u = np.asarray(x, np.float32).view(np.uint32)
    u = (u + np.uint32(0x7FFF) + ((u >> np.uint32(16)) & np.uint32(1))) & np.uint32(0xFFFF0000)
    return u.view(np.float32)


def _alibi_table():
    tab = np.zeros((DIFF_HEADS, SUBLANES, LANES), np.float32)
    for h in range(DIFF_HEADS):
        c = np.float32(2.0 ** (-8.0 * (h + 1) / DIFF_HEADS) * LOG2E)
        hi = _bf16_round(c)
        mid = _bf16_round(np.float32(c - hi))
        lo = _bf16_round(np.float32(c - hi - mid))
        tab[h, 0], tab[h, 1], tab[h, 2] = hi, mid, lo
        tab[h, 3] = np.float32(np.float32(hi + mid) + lo)
    return tab


def _trunk(x, layers, ctab, nf):
    B, S, _ = x.shape
    T = B * S
    x2d = x.reshape(T, D_MODEL)
    for l, p in enumerate(layers):
        lambda_init = 0.8 - 0.6 * math.exp(-0.3 * l)
        qkv, gate, ab, dqt, dk, dvt = _inproj(x2d, p["nmix"], p["wm"], p["wab"], p["wqt"], p["wvt"])
        r3 = lambda t: t.reshape(B, S, t.shape[-1])
        q, k, v, g = _gdn_prep(r3(qkv), r3(ab), p["cw"], p["gp"])
        u, w, qd, kd, at, gl = _gdn_intra(q, k, v, g)
        of, ob = _gdn_scan(u, w, qd, kd, at, gl)
        oa = _attention(dqt, r3(dk), dvt, ctab, p["lp"], p["dn"], lambda_init)
        x2d = _mix(x2d, of.reshape(T, GDN_WIDTH), ob.reshape(T, GDN_WIDTH), gate,
                   oa.reshape(T, DIFF_WIDTH), p["gn"], p["wo"])
        x2d = _mlp(x2d, p["nmlp"], p["wu"], p["wd"], nf, l == DEPTH - 1)
    return x2d.reshape(B, S, D_MODEL)


def kernel(x_prompt, x_sample, norm_mix, w_in, conv_w, a_log, dt_bias, gdn_norm, lambda_q1,
           lambda_k1, lambda_q2, lambda_k2, diff_norm, w_out, norm_mlp, w_up, w_down, norm_final):
    layers, ctab, nf = _prepare_weights(
        norm_mix, w_in, conv_w, a_log, dt_bias, gdn_norm, lambda_q1, lambda_k1,
        lambda_q2, lambda_k2, diff_norm, w_out, norm_mlp, w_up, w_down, norm_final)
    return (_trunk(x_prompt, layers, ctab, nf), _trunk(x_sample, layers, ctab, nf))
```

```python
import functools
import math

import jax
import jax.numpy as jnp
import numpy as np
from jax import lax
from jax.experimental import pallas as pl
from jax.experimental.pallas import tpu as pltpu

D_MODEL = 1024
DEPTH = 2
GDN_HEADS = 4
GDN_DK = 128
GDN_DV = 128
GDN_WIDTH = GDN_HEADS * GDN_DV
CONV_W = 5
CONV_PAD = CONV_W // 2
CHUNK = 64
DIFF_HEADS = 4
DIFF_DH = 64
DIFF_DV = 2 * DIFF_DH
DIFF_WIDTH = DIFF_HEADS * DIFF_DV
D_FF = 4 * D_MODEL
EPS = 1e-6
QKV_W = 3 * GDN_WIDTH
AB_LO = QKV_W + GDN_WIDTH
AB_HI = AB_LO + 4 * GDN_HEADS
LANES = 128
SUBLANES = 8
VMEM_LIMIT = 56 * 1024 * 1024
NEG_BIG = -1e30
LOG2E = math.log2(math.e)
Q_SCALE = DIFF_DH ** -0.5 * LOG2E
SCAN_UNITS = 64
DENSE_TOKENS = 512
GDN_TOKENS = 256
ATTN_QUERIES = 256
ATTN_KEYS = 512
ATTN_UNROLL = 15
BF16 = jnp.bfloat16
F32 = jnp.float32


def _dot(a, b):
    return jnp.dot(a, b, preferred_element_type=F32)


def _dot_nt(a, b):
    return lax.dot_general(a, b, (((1,), (1,)), ((), ())), preferred_element_type=F32)


def _dot_tn(a, b):
    return lax.dot_general(a, b, (((0,), (0,)), ((), ())), preferred_element_type=F32)


def _sigmoid(x):
    return 1.0 / (1.0 + jnp.exp(-x))


def _params(sem):
    return pltpu.CompilerParams(dimension_semantics=sem, vmem_limit_bytes=VMEM_LIMIT)


def _inproj_kernel(x_ref, nw_ref, wm_ref, wab_ref, wqt_ref, wvt_ref,
                   qkv_ref, gate_ref, ab_ref, dqt_ref, dk_ref, dvt_ref):
    x = x_ref[...]
    ms = jnp.mean(x * x, axis=-1, keepdims=True)
    h = (x * lax.rsqrt(ms + EPS) * nw_ref[...]).astype(BF16)
    qkv_ref[...] = _dot(h, wm_ref[:, 0:QKV_W])
    gate_ref[...] = _dot(h, wm_ref[:, QKV_W:AB_LO])
    dk_ref[...] = _dot(h, wm_ref[:, AB_LO:AB_LO + DIFF_WIDTH]).astype(BF16)
    ab_ref[...] = _dot(h, wab_ref[...])
    dqt_ref[...] = (_dot_nt(wqt_ref[...], h) * Q_SCALE).astype(BF16)
    dvt_ref[...] = _dot_nt(wvt_ref[...], h).astype(BF16)


def _inproj(x2d, nw, wm, wab, wqt, wvt):
    T = x2d.shape[0]
    tm = min(DENSE_TOKENS, T)
    row = lambda i: (i, 0)
    col = lambda i: (0, i)
    fixed = lambda i: (0, 0)
    nmain = wm.shape[1]
    return pl.pallas_call(
        _inproj_kernel,
        grid=(T // tm,),
        in_specs=[pl.BlockSpec((tm, D_MODEL), row),
                  pl.BlockSpec((1, D_MODEL), fixed),
                  pl.BlockSpec((D_MODEL, nmain), fixed),
                  pl.BlockSpec((D_MODEL, LANES), fixed),
                  pl.BlockSpec((DIFF_WIDTH, D_MODEL), fixed),
                  pl.BlockSpec((DIFF_WIDTH, D_MODEL), fixed)],
        out_specs=[pl.BlockSpec((tm, QKV_W), row),
                   pl.BlockSpec((tm, GDN_WIDTH), row),
                   pl.BlockSpec((tm, LANES), row),
                   pl.BlockSpec((DIFF_WIDTH, tm), col),
                   pl.BlockSpec((tm, DIFF_WIDTH), row),
                   pl.BlockSpec((DIFF_WIDTH, tm), col)],
        out_shape=[jax.ShapeDtypeStruct((T, QKV_W), F32),
                   jax.ShapeDtypeStruct((T, GDN_WIDTH), F32),
                   jax.ShapeDtypeStruct((T, LANES), F32),
                   jax.ShapeDtypeStruct((DIFF_WIDTH, T), BF16),
                   jax.ShapeDtypeStruct((T, DIFF_WIDTH), BF16),
                   jax.ShapeDtypeStruct((DIFF_WIDTH, T), BF16)],
        compiler_params=_params(("parallel",)),
        name="inproj",
    )(x2d, nw, wm, wab, wqt, wvt)


def _gdn_prep_kernel(cur_ref, prev_ref, next_ref, ab_ref, cw_ref, gp_ref,
                     q_ref, k_ref, v_ref, g_ref, xpad):
    i = pl.program_id(1)
    n = pl.num_programs(1)
    tm = cur_ref.shape[1]
    halo = SUBLANES
    xpad[halo:halo + tm, :] = cur_ref[0]
    xpad[0:halo, :] = jnp.where(i > 0, prev_ref[0], 0.0)
    xpad[halo + tm:2 * halo + tm, :] = jnp.where(i < n - 1, next_ref[0], 0.0)

    outs = (q_ref, k_ref, v_ref)
    for grp in range(3):
        c0 = grp * GDN_WIDTH
        acc = None
        for kk in range(CONV_W):
            r0 = halo - CONV_PAD + kk
            term = xpad[r0:r0 + tm, c0:c0 + GDN_WIDTH] * cw_ref[kk:kk + 1, c0:c0 + GDN_WIDTH]
            acc = term if acc is None else acc + term
        y = acc * _sigmoid(acc)
        if grp < 2:
            parts = []
            for h in range(GDN_HEADS):
                yh = y[:, h * GDN_DK:(h + 1) * GDN_DK]
                inv = lax.rsqrt(jnp.sum(yh * yh, axis=-1, keepdims=True) + EPS)
                if grp == 0:
                    inv = inv * (GDN_DK ** -0.5)
                parts.append(yh * inv)
            y = jnp.concatenate(parts, axis=1)
        outs[grp][0] = y

    ab = ab_ref[0]
    a_coef = -jnp.exp(gp_ref[0:1, :])
    z = ab + gp_ref[1:2, :]
    softplus = jnp.maximum(z, 0.0) + jnp.log1p(jnp.exp(-jnp.abs(z)))
    lane = lax.broadcasted_iota(jnp.int32, ab.shape, 1)
    ng = 2 * GDN_HEADS
    g_ref[0] = jnp.where(lane < ng, a_coef * softplus,
                         jnp.where(lane < 2 * ng, _sigmoid(ab), 0.0))


def _gdn_prep(qkv, ab, cw, gp):
    B, S, _ = qkv.shape
    tm = min(GDN_TOKENS, S)
    nb = tm // SUBLANES
    last = S // SUBLANES - 1
    tok = lambda b, i: (b, i, 0)
    fixed = lambda b, i: (0, 0)
    return pl.pallas_call(
        _gdn_prep_kernel,
        grid=(B, S // tm),
        in_specs=[pl.BlockSpec((1, tm, QKV_W), tok),
                  pl.BlockSpec((1, SUBLANES, QKV_W), lambda b, i: (b, jnp.maximum(i * nb - 1, 0), 0)),
                  pl.BlockSpec((1, SUBLANES, QKV_W), lambda b, i: (b, jnp.minimum((i + 1) * nb, last), 0)),
                  pl.BlockSpec((1, tm, LANES), tok),
                  pl.BlockSpec((CONV_W, QKV_W), fixed),
                  pl.BlockSpec((2, LANES), fixed)],
        out_specs=[pl.BlockSpec((1, tm, GDN_WIDTH), tok)] * 3 + [pl.BlockSpec((1, tm, LANES), tok)],
        out_shape=[jax.ShapeDtypeStruct((B, S, GDN_WIDTH), F32)] * 3
                  + [jax.ShapeDtypeStruct((B, S, LANES), F32)],
        scratch_shapes=[pltpu.VMEM((tm + 2 * SUBLANES, QKV_W), F32)],
        compiler_params=_params(("parallel", "parallel")),
        name="gdn_prep",
    )(qkv, qkv, qkv, ab, cw, gp)


def _gdn_intra_kernel(q_ref, k_ref, v_ref, g_ref,
                      u_ref, w_ref, qd_ref, kd_ref, at_ref, gl_ref):
    tc = q_ref.shape[1]
    C = CHUNK
    H = GDN_HEADS
    G = g_ref[0]
    pos = lax.broadcasted_iota(jnp.int32, G.shape, 0) & (C - 1)
    gcf = G
    gcb = G
    s = 1
    while s < C:
        gcf = gcf + jnp.where(pos >= s, pltpu.roll(gcf, s, axis=0), 0.0)
        gcb = gcb + jnp.where(pos < C - s, pltpu.roll(gcb, tc - s, axis=0), 0.0)
        s *= 2

    ri = lax.broadcasted_iota(jnp.int32, (C, 2 * C), 0)
    li = lax.broadcasted_iota(jnp.int32, (C, 2 * C), 1)
    ci = li & (C - 1)
    is_b = li >= C
    is_f = li < C
    incl = (is_f & (ci <= ri)) | (is_b & (ci >= ri))
    strict = (is_f & (ci < ri)) | (is_b & (ci > ri))
    colsum_mask = (is_f & (ri <= ci)) | (is_b & (ri >= ci))
    eye2 = (ri == ci).astype(F32)
    half_f = is_f.astype(BF16)
    half_b = is_b.astype(BF16)

    def block_diag(x):
        return jnp.concatenate([x * half_f, x * half_b], axis=0)

    heads = [(c, h) for c in range(tc // C) for h in range(H)]
    qs, ks, kkqk = {}, {}, {}
    for (c, h) in heads:
        r0 = c * C
        qs[c, h] = q_ref[0, r0:r0 + C, h * GDN_DK:(h + 1) * GDN_DK]
        ks[c, h] = k_ref[0, r0:r0 + C, h * GDN_DK:(h + 1) * GDN_DK]
    for (c, h) in heads:
        kb16 = ks[c, h].astype(BF16)
        kq = jnp.concatenate([kb16, qs[c, h].astype(BF16)], axis=0)
        kkqk[c, h] = _dot_nt(kq, jnp.concatenate([kb16, kb16], axis=0))

    pmat, amat, rhs = {}, {}, {}
    for (c, h) in heads:
        r0 = c * C
        rows = slice(r0, r0 + C)
        kh = ks[c, h]
        vh = v_ref[0, rows, h * GDN_DV:(h + 1) * GDN_DV]
        cols = {}
        for d in range(2):
            ln = d * H + h
            gc_all = gcf if d == 0 else gcb
            gc_col = gc_all[rows, ln:ln + 1]
            beta_col = G[rows, 2 * H + ln:2 * H + ln + 1]
            last = C - 1 if d == 0 else 0
            gc_last = gc_all[r0 + last:r0 + last + 1, ln:ln + 1]
            e_col = jnp.exp(gc_col)
            cols[d] = (G[rows, ln:ln + 1], gc_col, beta_col)
            rhs[c, h, d] = jnp.concatenate([vh * beta_col, kh * (beta_col * e_col)], axis=1).astype(BF16)
            lo = ln * GDN_DK
            qd_ref[0, rows, lo:lo + GDN_DK] = (qs[c, h] * e_col).astype(BF16)
            kd_ref[0, rows, lo:lo + GDN_DK] = (kh * jnp.exp(gc_last - gc_col)).astype(BF16)
            gl_ref[0, c, ln:ln + 1, :] = jnp.broadcast_to(jnp.exp(gc_last), (1, LANES))
        g2, gc2, beta2 = (jnp.where(is_b, cols[1][t], cols[0][t]) for t in range(3))
        gc_row = jnp.sum(jnp.where(colsum_mask, g2, 0.0), axis=0, keepdims=True)
        decay = jnp.exp(jnp.where(incl, gc2 - gc_row, NEG_BIG))
        m1 = kkqk[c, h]
        amat[c, h] = jnp.where(strict, (m1[:C] * beta2) * decay, 0.0)
        at_ref[0, rows, h * 2 * C:(h + 1) * 2 * C] = (m1[C:] * decay).astype(BF16)

    a16 = {}
    for hd in heads:
        pmat[hd] = eye2 - jnp.where((ri >> 1) == (ci >> 1), amat[hd], 0.0)
        a16[hd] = amat[hd].astype(BF16)
    lb = 1
    while (1 << lb) < C:
        coupling = (((ri >> (lb + 1)) == (ci >> (lb + 1))) & ((ri >> lb) != (ci >> lb))).astype(BF16)
        xs, p16 = {}, {}
        for hd in heads:
            p16[hd] = pmat[hd].astype(BF16)
            xs[hd] = _dot(a16[hd] * coupling, block_diag(p16[hd])).astype(BF16)
        for hd in heads:
            pmat[hd] = pmat[hd] - _dot(p16[hd], block_diag(xs[hd]))
        lb += 1

    zero_rhs = jnp.zeros((C, GDN_DV + GDN_DK), BF16)
    for (c, h) in heads:
        rows = slice(c * C, (c + 1) * C)
        both = jnp.concatenate([jnp.concatenate([rhs[c, h, 0], zero_rhs], axis=1),
                                jnp.concatenate([zero_rhs, rhs[c, h, 1]], axis=1)], axis=0)
        sol = _dot(pmat[c, h].astype(BF16), both)
        for d in range(2):
            lo = (d * H + h) * GDN_DK
            so = d * (GDN_DV + GDN_DK)
            u_ref[0, rows, lo:lo + GDN_DV] = sol[:, so:so + GDN_DV].astype(BF16)
            w_ref[0, rows, lo:lo + GDN_DK] = sol[:, so + GDN_DV:so + GDN_DV + GDN_DK].astype(BF16)


def _gdn_intra(q, k, v, g):
    B, S, _ = q.shape
    tc = min(GDN_TOKENS, S)
    tok = lambda b, i: (b, i, 0)
    wide = 2 * GDN_WIDTH
    return pl.pallas_call(
        _gdn_intra_kernel,
        grid=(B, S // tc),
        in_specs=[pl.BlockSpec((1, tc, GDN_WIDTH), tok)] * 3 + [pl.BlockSpec((1, tc, LANES), tok)],
        out_specs=[pl.BlockSpec((1, tc, wide), tok)] * 4
                  + [pl.BlockSpec((1, tc, 2 * GDN_HEADS * CHUNK), tok),
                     pl.BlockSpec((1, tc // CHUNK, 2 * GDN_HEADS, LANES), lambda b, i: (b, i, 0, 0))],
        out_shape=[jax.ShapeDtypeStruct((B, S, wide), BF16)] * 4
                  + [jax.ShapeDtypeStruct((B, S, 2 * GDN_HEADS * CHUNK), BF16),
                     jax.ShapeDtypeStruct((B, S // CHUNK, 2 * GDN_HEADS, LANES), F32)],
        compiler_params=_params(("parallel", "parallel")),
        name="gdn_intra",
    )(q, k, v, g)


def _gdn_scan_kernel(uf_ref, ub_ref, wf_ref, wb_ref, qf_ref, qb_ref, kf_ref, kb_ref,
                     af_ref, ab_ref, gf_ref, gb_ref, of_ref, ob_ref, st_ref):
    n = pl.program_id(0)
    B, tb = uf_ref.shape[0], uf_ref.shape[1]
    C = CHUNK
    H = GDN_HEADS
    nc = tb // C

    @pl.when(n == 0)
    def _():
        st_ref[...] = jnp.zeros_like(st_ref)

    refs = ((uf_ref, wf_ref, qf_ref, kf_ref, af_ref, gf_ref, of_ref),
            (ub_ref, wb_ref, qb_ref, kb_ref, ab_ref, gb_ref, ob_ref))
    chains = [(b, d, h) for b in range(B) for d in range(2) for h in range(H)]
    for cc in range(nc):
        states, ms, vbs = {}, {}, {}
        for ch in chains:
            b, d, h = ch
            u_ref, w_ref, qd_ref, kd_ref, at_ref, gl_ref, o_ref = refs[d]
            r0 = (cc if d == 0 else nc - 1 - cc) * C
            cols = slice(h * GDN_DK, (h + 1) * GDN_DK)
            states[ch] = st_ref[(b * 2 + d) * H + h]
            wq = jnp.concatenate([w_ref[b, r0:r0 + C, cols], qd_ref[b, r0:r0 + C, cols]], axis=0)
            ms[ch] = _dot(wq, states[ch].astype(BF16))
        for ch in chains:
            b, d, h = ch
            u_ref = refs[d][0]
            r0 = (cc if d == 0 else nc - 1 - cc) * C
            cols = slice(h * GDN_DK, (h + 1) * GDN_DK)
            vbs[ch] = (u_ref[b, r0:r0 + C, cols] - ms[ch][:C]).astype(BF16)
        for ch in chains:
            b, d, h = ch
            u_ref, w_ref, qd_ref, kd_ref, at_ref, gl_ref, o_ref = refs[d]
            c = cc if d == 0 else nc - 1 - cc
            r0 = c * C
            ln = d * H + h
            cols = slice(h * GDN_DK, (h + 1) * GDN_DK)
            ac = (2 * h + d) * C
            o = ms[ch][C:] + _dot(at_ref[b, r0:r0 + C, ac:ac + C], vbs[ch])
            o_ref[b, r0:r0 + C, cols] = o.astype(o_ref.dtype)
            gl = gl_ref[b, c, ln:ln + 1, :]
            st_ref[(b * 2 + d) * H + h] = states[ch] * gl + _dot_tn(kd_ref[b, r0:r0 + C, cols], vbs[ch])


def _gdn_scan(u, w, qd, kd, at, gl):
    B, S, _ = u.shape
    tb = min(SCAN_UNITS // (B * 2 * GDN_HEADS) * CHUNK, S)
    nb = S // tb
    nc = tb // CHUNK
    fwd = lambda n: (0, n, 0)
    bwd = lambda n: (0, nb - 1 - n, 1)
    fwd4 = lambda n: (0, n, 0, 0)
    bwd4 = lambda n: (0, nb - 1 - n, 0, 0)
    wide = GDN_WIDTH
    aw = 2 * GDN_HEADS * CHUNK
    pair = lambda shape: [pl.BlockSpec(shape, fwd), pl.BlockSpec(shape, bwd)]
    glshape = (B, nc, 2 * GDN_HEADS, LANES)
    return pl.pallas_call(
        _gdn_scan_kernel,
        grid=(nb,),
        in_specs=pair((B, tb, wide)) * 4
                 + [pl.BlockSpec((B, tb, aw), fwd), pl.BlockSpec((B, tb, aw), lambda n: (0, nb - 1 - n, 0))]
                 + [pl.BlockSpec(glshape, fwd4), pl.BlockSpec(glshape, bwd4)],
        out_specs=[pl.BlockSpec((B, tb, wide), fwd),
                   pl.BlockSpec((B, tb, wide), lambda n: (0, nb - 1 - n, 0))],
        out_shape=[jax.ShapeDtypeStruct((B, S, wide), BF16)] * 2,
        scratch_shapes=[pltpu.VMEM((B * 2 * GDN_HEADS, GDN_DK, GDN_DV), F32)],
        compiler_params=_params(("arbitrary",)),
        name="gdn_scan",
    )(u, u, w, w, qd, qd, kd, kd, at, at, gl, gl)


def _attn_kernel(lambda_init, tk, qt_ref, k_ref, vt_ref, ct_ref, lp_ref, dn_ref, o_ref,
                 w_scr, pk_scr, sa_scr, sb_scr, pa_scr, pb_scr, acc_scr):
    i = pl.program_id(2)
    tq = qt_ref.shape[1]
    S = k_ref.shape[1]
    nk = S // tk
    half = tk // 2
    wide = 2 * tq

    qt = qt_ref[...]
    row = lax.broadcasted_iota(jnp.int32, qt.shape, 0)
    zero = jnp.zeros_like(qt)
    top = jnp.concatenate([jnp.where(row < DIFF_DH, qt, zero),
                           jnp.where(row >= DIFF_DH, qt, zero)], axis=1)
    ct = ct_ref[0]
    rep = wide // LANES
    rw = lax.broadcasted_iota(jnp.int32, (LANES, wide), 0)
    pos = jnp.where(rw == 0, jnp.tile(ct[0:1], (1, rep)),
                    jnp.where(rw == 1, jnp.tile(ct[1:2], (1, rep)),
                              jnp.where(rw == 2, jnp.tile(ct[2:3], (1, rep)), 0.0)))
    for kind, sign in enumerate((1.0, 0.0, -1.0)):
        w_scr[kind, 0:LANES, :] = top
        w_scr[kind, LANES:2 * LANES, :] = (sign * pos).astype(BF16)
    kr = lax.broadcasted_iota(jnp.int32, (tk, LANES), 0)
    kl = lax.broadcasted_iota(jnp.int32, (tk, LANES), 1)
    pk_scr[...] = jnp.where(kl < 3, (kr - half).astype(F32), 0.0).astype(BF16)
    acc_scr[...] = jnp.zeros_like(acc_scr)
    c = ct[3:4, 0:1]
    li = lax.broadcasted_iota(jnp.int32, (1, wide), 1)
    tpos = (jnp.where(li >= tq, li - tq, li) + i * tq).astype(F32)
    jd = (i * tq) // tk

    def scores(j, kind):
        j0 = pl.multiple_of(j * tk, tk)
        ka = jnp.concatenate([k_ref[0, pl.ds(j0, tk), :], pk_scr[...]], axis=1)
        return _dot(ka, w_scr[kind])

    def pv(j, p_ref):
        j0 = pl.multiple_of(j * tk, tk)
        return _dot(vt_ref[:, pl.ds(j0, tk)], p_ref[...])

    kpos = (jd * tk + lax.broadcasted_iota(jnp.int32, (tk, tq), 0)).astype(F32)
    qpos = (i * tq + lax.broadcasted_iota(jnp.int32, (tk, tq), 1)).astype(F32)
    bias = -c * jnp.abs(qpos - kpos)
    s = scores(jd, 1) + jnp.concatenate([bias, bias], axis=1)
    m0 = jnp.max(s, axis=0, keepdims=True)
    p = jnp.exp2(s - m0)
    l0 = jnp.sum(p, axis=0, keepdims=True)
    pb_scr[...] = p.astype(BF16)

    def tile(t):
        return jnp.where(t >= jd, t + 1, t)

    def kind_of(j):
        return jnp.where(j < jd, 0, 2)

    def half_step(t, s_cur, s_nxt, p_cur, p_prev, carry):
        m_run, l_run, alpha_prev, smax = carry
        j = tile(t)
        smax_nxt = smax
        if s_nxt is not None:
            jn = tile(t + 1)
            s_val = scores(jn, kind_of(jn))
            s_nxt[...] = s_val
            smax_nxt = jnp.max(s_val, axis=0, keepdims=True)
        sigma = jnp.where(j < jd, -1.0, 1.0).astype(F32)
        su = (sigma * c) * (tpos - (j * tk + half).astype(F32))
        m_new = jnp.maximum(m_run, smax + su)
        p = jnp.exp2(s_cur[...] - (m_new - su))
        alpha = jnp.exp2(m_run - m_new)
        l_new = alpha * l_run + jnp.sum(p, axis=0, keepdims=True)
        p_cur[...] = p.astype(BF16)
        jp = jnp.where(t == 0, jd, tile(t - 1))
        acc_scr[...] = acc_scr[...] * alpha_prev + pv(jp, p_prev)
        return m_new, l_new, alpha, smax_nxt

    def pair(pp, carry):
        carry = half_step(2 * pp, sa_scr, sb_scr, pa_scr, pb_scr, carry)
        return half_step(2 * pp + 1, sb_scr, sa_scr, pb_scr, pa_scr, carry)

    j_first = tile(0)
    s_first = scores(j_first, kind_of(j_first))
    sa_scr[...] = s_first
    carry = (m0, l0, jnp.ones((1, wide), F32), jnp.max(s_first, axis=0, keepdims=True))
    npairs = (nk - 2) // 2
    unroll = max(u for u in range(1, ATTN_UNROLL + 1) if npairs % u == 0) if npairs else 1
    carry = lax.fori_loop(0, npairs, pair, carry, unroll=unroll)
    _, l_run, alpha_last, _ = half_step(nk - 2, sa_scr, None, pa_scr, pb_scr, carry)
    acc_scr[...] = acc_scr[...] * alpha_last + pv(tile(nk - 2), pa_scr)

    lp = lp_ref[...]
    lam = (jnp.exp(jnp.sum(lp[0:1] * lp[1:2], axis=1, keepdims=True))
           - jnp.exp(jnp.sum(lp[2:3] * lp[3:4], axis=1, keepdims=True)) + lambda_init)
    o_all = acc_scr[...] / l_run
    o = o_all[:, :tq] - lam * o_all[:, tq:]
    o = o * lax.rsqrt(jnp.mean(o * o, axis=0, keepdims=True) + EPS) * jnp.tile(dn_ref[...], (1, tq // LANES))
    o_ref[0] = (o * (1.0 - lambda_init)).T.astype(o_ref.dtype)


def _attention(dqt, dk, dvt, ctab, lp, dn, lambda_init):
    B, S, _ = dk.shape
    tk = min(ATTN_KEYS, S // 2)
    tq = min(ATTN_QUERIES, tk)
    assert tk % tq == 0 and tk // 2 <= 256 and (S // tk) % 2 == 0
    nq = S // tq
    return pl.pallas_call(
        functools.partial(_attn_kernel, lambda_init, tk),
        grid=(B, DIFF_HEADS, nq),
        in_specs=[pl.BlockSpec((DIFF_DV, tq), lambda b, h, i: (h, b * nq + i)),
                  pl.BlockSpec((1, S, DIFF_DV), lambda b, h, i: (b, 0, h)),
                  pl.BlockSpec((DIFF_DV, S), lambda b, h, i: (h, b)),
                  pl.BlockSpec((1, SUBLANES, LANES), lambda b, h, i: (h, 0, 0)),
                  pl.BlockSpec((4, DIFF_DH), lambda b, h, i: (0, 0)),
                  pl.BlockSpec((DIFF_DV, LANES), lambda b, h, i: (0, 0))],
        out_specs=pl.BlockSpec((1, tq, DIFF_DV), lambda b, h, i: (b, i, h)),
        out_shape=jax.ShapeDtypeStruct((B, S, DIFF_WIDTH), BF16),
        scratch_shapes=[pltpu.VMEM((3, 2 * LANES, 2 * tq), BF16),
                        pltpu.VMEM((tk, LANES), BF16),
                        pltpu.VMEM((tk, 2 * tq), F32),
                        pltpu.VMEM((tk, 2 * tq), F32),
                        pltpu.VMEM((tk, 2 * tq), BF16),
                        pltpu.VMEM((tk, 2 * tq), BF16),
                        pltpu.VMEM((DIFF_DV, 2 * tq), F32)],
        compiler_params=_params(("parallel", "parallel", "parallel")),
        name="diff_attn",
    )(dqt, dk, dvt, ctab, lp, dn)


def _mix_kernel(x_ref, of_ref, ob_ref, gate_ref, oa_ref, gn_ref, wo_ref, y_ref):
    o = of_ref[...].astype(F32) + ob_ref[...].astype(F32)
    gate = gate_ref[...]
    parts = []
    for h in range(GDN_HEADS):
        cols = slice(h * GDN_DV, (h + 1) * GDN_DV)
        oh = o[:, cols]
        oh = oh * lax.rsqrt(jnp.mean(oh * oh, axis=-1, keepdims=True) + EPS) * gn_ref[...]
        gh = gate[:, cols]
        parts.append(oh * (gh * _sigmoid(gh)))
    og = jnp.concatenate(parts, axis=1).astype(BF16)
    y_ref[...] = (x_ref[...] + _dot(og, wo_ref[0:GDN_WIDTH, :])
                  + _dot(oa_ref[...], wo_ref[GDN_WIDTH:GDN_WIDTH + DIFF_WIDTH, :]))


def _mix(x2d, of, ob, gate, oa, gn, wo):
    T = x2d.shape[0]
    tm = min(DENSE_TOKENS, T)
    row = lambda i: (i, 0)
    fixed = lambda i: (0, 0)
    half = pl.BlockSpec((tm, GDN_WIDTH), row)
    return pl.pallas_call(
        _mix_kernel,
        grid=(T // tm,),
        in_specs=[pl.BlockSpec((tm, D_MODEL), row), half, half, half, half,
                  pl.BlockSpec((1, GDN_DV), fixed),
                  pl.BlockSpec((GDN_WIDTH + DIFF_WIDTH, D_MODEL), fixed)],
        out_specs=pl.BlockSpec((tm, D_MODEL), row),
        out_shape=jax.ShapeDtypeStruct((T, D_MODEL), F32),
        compiler_params=_params(("parallel",)),
        name="mix_out",
    )(x2d, of, ob, gate, oa, gn, wo)


def _mlp_kernel(final, x_ref, nw_ref, wu_ref, wd_ref, nf_ref, y_ref):
    x = x_ref[...]
    ms = jnp.mean(x * x, axis=-1, keepdims=True)
    h = (x * lax.rsqrt(ms + EPS) * nw_ref[...]).astype(BF16)
    acc = x
    step = D_MODEL
    for c in range(D_FF // step):
        up = _dot(h, wu_ref[:, c * step:(c + 1) * step])
        act = jnp.square(jnp.maximum(up, 0.0)).astype(BF16)
        acc = acc + _dot(act, wd_ref[c * step:(c + 1) * step, :])
    if final:
        ms2 = jnp.mean(acc * acc, axis=-1, keepdims=True)
        acc = acc * lax.rsqrt(ms2 + EPS) * nf_ref[...]
    y_ref[...] = acc


def _mlp(x2d, nw, wu, wd, nf, final):
    T = x2d.shape[0]
    tm = min(DENSE_TOKENS, T)
    row = lambda i: (i, 0)
    fixed = lambda i: (0, 0)
    return pl.pallas_call(
        functools.partial(_mlp_kernel, final),
        grid=(T // tm,),
        in_specs=[pl.BlockSpec((tm, D_MODEL), row),
                  pl.BlockSpec((1, D_MODEL), fixed),
                  pl.BlockSpec((D_MODEL, D_FF), fixed),
                  pl.BlockSpec((D_FF, D_MODEL), fixed),
                  pl.BlockSpec((1, D_MODEL), fixed)],
        out_specs=pl.BlockSpec((tm, D_MODEL), row),
        out_shape=jax.ShapeDtypeStruct((T, D_MODEL), F32),
        compiler_params=_params(("parallel",)),
        name="mlp",
    )(x2d, nw, wu, wd, nf)


def _prepare_weights(norm_mix, w_in, conv_w, a_log, dt_bias, gdn_norm, lambda_q1, lambda_k1,
                     lambda_q2, lambda_k2, diff_norm, w_out, norm_mlp, w_up, w_down, norm_final):
    layers = []
    ngate = 2 * GDN_HEADS
    q_lo, k_lo, v_lo = AB_HI, AB_HI + DIFF_WIDTH, AB_HI + 2 * DIFF_WIDTH
    for l in range(DEPTH):
        wm = jnp.concatenate([w_in[l][:, :AB_LO], w_in[l][:, k_lo:v_lo]], axis=1).astype(BF16)
        wab = jnp.pad(w_in[l][:, AB_LO:AB_HI], ((0, 0), (0, LANES - (AB_HI - AB_LO)))).astype(BF16)
        wqt = w_in[l][:, q_lo:k_lo].T.astype(BF16)
        wvt = w_in[l][:, v_lo:v_lo + DIFF_WIDTH].T.astype(BF16)
        gp = jnp.stack([jnp.pad(a_log[l].reshape(ngate).astype(F32), (0, LANES - ngate)),
                        jnp.pad(dt_bias[l].reshape(ngate).astype(F32), (0, LANES - ngate))])
        lp = jnp.stack([lambda_q1[l], lambda_k1[l], lambda_q2[l], lambda_k2[l]]).astype(F32)
        layers.append(dict(
            nmix=norm_mix[l].reshape(1, D_MODEL), wm=wm, wab=wab, wqt=wqt, wvt=wvt,
            cw=conv_w[l], gp=gp, gn=gdn_norm[l].reshape(1, GDN_DV),
            lp=lp, dn=jnp.broadcast_to(diff_norm[l].astype(F32)[:, None], (DIFF_DV, LANES)),
            wo=w_out[l].astype(BF16), nmlp=norm_mlp[l].reshape(1, D_MODEL),
            wu=w_up[l].astype(BF16), wd=w_down[l].astype(BF16)))
    return layers, jnp.asarray(_alibi_table()), norm_final.reshape(1, D_MODEL)


def _bf16_round(x):
    u = np.asarray(x, np.float32).view(np.uint32)
    u = (u + np.uint32(0x7FFF) + ((u >> np.uint32(16)) & np.uint32(1))) & np.uint32(0xFFFF0000)
    return u.view(np.float32)


def _alibi_table():
    tab = np.zeros((DIFF_HEADS, SUBLANES, LANES), np.float32)
    for h in range(DIFF_HEADS):
        c = np.float32(2.0 ** (-8.0 * (h + 1) / DIFF_HEADS) * LOG2E)
        hi = _bf16_round(c)
        mid = _bf16_round(np.float32(c - hi))
        lo = _bf16_round(np.float32(c - hi - mid))
        tab[h, 0], tab[h, 1], tab[h, 2] = hi, mid, lo
        tab[h, 3] = np.float32(np.float32(hi + mid) + lo)
    return tab


def _trunk(x, layers, ctab, nf):
    B, S, _ = x.shape
    T = B * S
    x2d = x.reshape(T, D_MODEL)
    for l, p in enumerate(layers):
        lambda_init = 0.8 - 0.6 * math.exp(-0.3 * l)
        qkv, gate, ab, dqt, dk, dvt = _inproj(x2d, p["nmix"], p["wm"], p["wab"], p["wqt"], p["wvt"])
        r3 = lambda t: t.reshape(B, S, t.shape[-1])
        q, k, v, g = _gdn_prep(r3(qkv), r3(ab), p["cw"], p["gp"])
        u, w, qd, kd, at, gl = _gdn_intra(q, k, v, g)
        of, ob = _gdn_scan(u, w, qd, kd, at, gl)
        oa = _attention(dqt, r3(dk), dvt, ctab, p["lp"], p["dn"], lambda_init)
        x2d = _mix(x2d, of.reshape(T, GDN_WIDTH), ob.reshape(T, GDN_WIDTH), gate,
                   oa.reshape(T, DIFF_WIDTH), p["gn"], p["wo"])
        x2d = _mlp(x2d, p["nmlp"], p["wu"], p["wd"], nf, l == DEPTH - 1)
    return x2d.reshape(B, S, D_MODEL)


def kernel(x_prompt, x_sample, norm_mix, w_in, conv_w, a_log, dt_bias, gdn_norm, lambda_q1,
           lambda_k1, lambda_q2, lambda_k2, diff_norm, w_out, norm_mlp, w_up, w_down, norm_final):
    layers, ctab, nf = _prepare_weights(
        norm_mix, w_in, conv_w, a_log, dt_bias, gdn_norm, lambda_q1, lambda_k1,
        lambda_q2, lambda_k2, diff_norm, w_out, norm_mlp, w_up, w_down, norm_final)
    return (_trunk(x_prompt, layers, ctab, nf), _trunk(x_sample, layers, ctab, nf))
```
